```python
import math
import jax, jax.numpy as jnp
from jax import lax
import numpy as np

D_MODEL = 1024
BATCH = 32
SEQ = 2048
DEPTH = 1

N_HEADS = 8
QK_NOPE_DIM = 64
QK_ROPE_DIM = 32
QK_DIM = QK_NOPE_DIM + QK_ROPE_DIM
V_HEAD_DIM = 64
Q_LORA_RANK = 256
KV_LORA_RANK = 128
ROPE_THETA = 10000.0
Q_BLOCK = 128
ATTN_WIDTH = N_HEADS * V_HEAD_DIM

LRU_WIDTH = 512
LRU_BLOCKS = 4
LRU_BLOCK_DIM = LRU_WIDTH // LRU_BLOCKS
CONV_WIDTH = 4
CONV_LEFT = 2
LRU_C = 8.0
N_DIRS = 2

N_BRANCHES = 2
D_FF = -(-8 * D_MODEL // (3 * 256)) * 256
NORM_EPS = 1e-6

C_QA = Q_LORA_RANK
C_KVA = C_QA + KV_LORA_RANK
C_KR = C_KVA + QK_ROPE_DIM
C_LX = C_KR + LRU_WIDTH
C_LY = C_LX + LRU_WIDTH
D_IN = C_LY + N_BRANCHES * D_MODEL

kernel_name = 'hybrid_mla_rglru_encoder'


def rmsnorm(x, g):
    xf = x.astype(jnp.float32)
    y = xf * lax.rsqrt(jnp.mean(xf * xf, axis=-1, keepdims=True) + NORM_EPS)
    return (y * g.astype(jnp.float32)).astype(x.dtype)


def rope_tables(seq):
    pos = jnp.arange(seq, dtype=jnp.float32)
    inv_freq = 1.0 / (ROPE_THETA ** (jnp.arange(0, QK_ROPE_DIM, 2, dtype=jnp.float32) / QK_ROPE_DIM))
    ang = pos[:, None] * inv_freq[None, :]
    return jnp.cos(ang), jnp.sin(ang)


def apply_rope(x, cos, sin):
    x1, x2 = jnp.split(x, 2, axis=-1)
    cos = cos.astype(x.dtype)
    sin = sin.astype(x.dtype)
    return jnp.concatenate([x1 * cos - x2 * sin, x2 * cos + x1 * sin], axis=-1)


def mla_mixer(q_a, kv_a, k_rope_raw, q_a_norm_g, w_q_b, kv_a_norm_g, w_kv_b):
    b, s, _ = q_a.shape
    cos, sin = rope_tables(s)
    q = (rmsnorm(q_a, q_a_norm_g) @ w_q_b).reshape(b, s, N_HEADS, QK_DIM)
    q_nope, q_rope = q[..., :QK_NOPE_DIM], q[..., QK_NOPE_DIM:]
    q_rope = apply_rope(q_rope, cos[:, None, :], sin[:, None, :])
    kv = (rmsnorm(kv_a, kv_a_norm_g) @ w_kv_b).reshape(b, s, N_HEADS, QK_NOPE_DIM + V_HEAD_DIM)
    k_nope, v = kv[..., :QK_NOPE_DIM], kv[..., QK_NOPE_DIM:]
    k_rope = apply_rope(k_rope_raw, cos, sin)
    q = jnp.concatenate([q_nope, q_rope], axis=-1)
    k = jnp.concatenate([k_nope, jnp.broadcast_to(k_rope[:, :, None, :], (b, s, N_HEADS, QK_ROPE_DIM))], axis=-1)
    scale = QK_DIM ** -0.5
    n_blk = s // Q_BLOCK
    q_blocks = q.reshape(b, n_blk, Q_BLOCK, N_HEADS, QK_DIM).transpose(1, 0, 2, 3, 4)

    def attend(qb):
        sc = jnp.einsum('bqhd,bkhd->bhqk', qb, k).astype(jnp.float32) * scale
        p = jax.nn.softmax(sc, axis=-1).astype(v.dtype)
        return jnp.einsum('bhqk,bkhd->bqhd', p, v)

    o = lax.map(attend, q_blocks)
    return o.transpose(1, 0, 2, 3, 4).reshape(b, s, ATTN_WIDTH)


def _lin_combine(left, right):
    a1, b1 = left
    a2, b2 = right
    return a1 * a2, a2 * b1 + b2


def rglru_bidir(xl, conv_w, conv_b, w_r, b_r, w_i, b_i, lam):
    b, s, w = xl.shape
    xp = jnp.pad(xl, ((0, 0), (CONV_LEFT, CONV_WIDTH - 1 - CONV_LEFT), (0, 0)))
    xc = sum(xp[:, j:j + s, :] * conv_w[j] for j in range(CONV_WIDTH)) + conv_b
    xb = xc.reshape(b, s, LRU_BLOCKS, LRU_BLOCK_DIM)
    r = jax.nn.sigmoid((jnp.einsum('bsnc,rncd->rbsnd', xb, w_r) + b_r[:, None, None]).astype(jnp.float32))
    i = jax.nn.sigmoid((jnp.einsum('bsnc,rncd->rbsnd', xb, w_i) + b_i[:, None, None]).astype(jnp.float32))
    r = r.reshape(N_DIRS, b, s, w)
    i = i.reshape(N_DIRS, b, s, w)
    log_a = -LRU_C * r * jax.nn.softplus(-lam.astype(jnp.float32))[:, None, None, :]
    a = jnp.exp(log_a)
    mult = jnp.sqrt(-jnp.expm1(2.0 * log_a))
    u = i * xc.astype(jnp.float32)[None] * mult
    _, h_fwd = lax.associative_scan(_lin_combine, (a[0], u[0]), axis=1)
    _, h_bwd = lax.associative_scan(_lin_combine, (a[1], u[1]), axis=1, reverse=True)
    return (h_fwd + h_bwd).astype(xl.dtype)


def setup_inputs(seed: int = 0) -> dict:
    key = jax.random.key(seed)
    ks = jax.random.split(key, 24)
    f32 = jnp.float32
    L = DEPTH

    def nrm(k, shape, fan_in):
        return jax.random.normal(k, shape, f32) * (fan_in ** -0.5)

    def gain(k, shape):
        return jnp.ones(shape, f32) + 0.01 * jax.random.normal(k, shape, f32)

    u = jax.random.uniform(ks[14], (L, N_DIRS, LRU_WIDTH), f32, 0.9, 0.999)
    a0 = u ** (1.0 / LRU_C)
    lru_lambda = jnp.log(a0) - jnp.log1p(-a0)
    return {
        'x': jax.random.normal(ks[0], (BATCH, SEQ, D_MODEL), f32),
        'norm1_g': gain(ks[1], (L, D_MODEL)),
        'w_in': nrm(ks[2], (L, D_MODEL, D_IN), D_MODEL),
        'q_a_norm_g': gain(ks[3], (L, Q_LORA_RANK)),
        'w_q_b': nrm(ks[4], (L, Q_LORA_RANK, N_HEADS * QK_DIM), Q_LORA_RANK),
        'kv_a_norm_g': gain(ks[5], (L, KV_LORA_RANK)),
        'w_kv_b': nrm(ks[6], (L, KV_LORA_RANK, N_HEADS * (QK_NOPE_DIM + V_HEAD_DIM)), KV_LORA_RANK),
        'w_o_attn': nrm(ks[7], (L, ATTN_WIDTH, D_MODEL), ATTN_WIDTH),
        'conv_w': nrm(ks[8], (L, CONV_WIDTH, LRU_WIDTH), CONV_WIDTH),
        'conv_b': 0.1 * jax.random.normal(ks[9], (L, LRU_WIDTH), f32),
        'w_rgate': nrm(ks[10], (L, N_DIRS, LRU_BLOCKS, LRU_BLOCK_DIM, LRU_BLOCK_DIM), LRU_BLOCK_DIM),
        'b_rgate': 0.1 * jax.random.normal(ks[11], (L, N_DIRS, LRU_BLOCKS, LRU_BLOCK_DIM), f32),
        'w_igate': nrm(ks[12], (L, N_DIRS, LRU_BLOCKS, LRU_BLOCK_DIM, LRU_BLOCK_DIM), LRU_BLOCK_DIM),
        'b_igate': 0.1 * jax.random.normal(ks[13], (L, N_DIRS, LRU_BLOCKS, LRU_BLOCK_DIM), f32),
        'lru_lambda': lru_lambda,
        'w_o_lru': nrm(ks[15], (L, LRU_WIDTH, D_MODEL), LRU_WIDTH),
        'w_out': nrm(ks[16], (L, D_MODEL, D_MODEL), D_MODEL),
        'norm2_g': gain(ks[17], (L, D_MODEL)),
        'w_ffn_gate': nrm(ks[18], (L, D_MODEL, D_FF), D_MODEL),
        'w_ffn_up': nrm(ks[19], (L, D_MODEL, D_FF), D_MODEL),
        'w_ffn_down': nrm(ks[20], (L, D_FF, D_MODEL), D_FF),
        'final_g': gain(ks[21], (D_MODEL,)),
    }


def reference(x, norm1_g, w_in, q_a_norm_g, w_q_b, kv_a_norm_g, w_kv_b, w_o_attn,
              conv_w, conv_b, w_rgate, b_rgate, w_igate, b_igate, lru_lambda, w_o_lru,
              w_out, norm2_g, w_ffn_gate, w_ffn_up, w_ffn_down, final_g):
    for l in range(DEPTH):
        h = rmsnorm(x, norm1_g[l])
        proj = h @ w_in[l]
        q_a = proj[..., :C_QA]
        kv_a = proj[..., C_QA:C_KVA]
        k_rope_raw = proj[..., C_KVA:C_KR]
        lru_x = proj[..., C_KR:C_LX]
        lru_y = proj[..., C_LX:C_LY]
        gate_logits = proj[..., C_LY:]

        attn = mla_mixer(q_a, kv_a, k_rope_raw, q_a_norm_g[l], w_q_b[l], kv_a_norm_g[l], w_kv_b[l]) @ w_o_attn[l]
        rec = rglru_bidir(lru_x, conv_w[l], conv_b[l], w_rgate[l], b_rgate[l], w_igate[l], b_igate[l], lru_lambda[l])
        rec = (jax.nn.gelu(lru_y) * rec) @ w_o_lru[l]

        gates = jax.nn.sigmoid(gate_logits.astype(jnp.float32)).astype(x.dtype)
        g_attn = gates[..., :D_MODEL]
        g_rec = gates[..., D_MODEL:]
        merged = g_attn * attn + g_rec * rec
        x = x + merged @ w_out[l]

        h2 = rmsnorm(x, norm2_g[l])
        x = x + (jax.nn.silu(h2 @ w_ffn_gate[l]) * (h2 @ w_ffn_up[l])) @ w_ffn_down[l]
    return rmsnorm(x, final_g)
```

```python
import functools

import jax
import jax.numpy as jnp
from jax import lax
from jax.experimental import pallas as pl
from jax.experimental.pallas import tpu as pltpu

N_HEADS = 8
QK_NOPE_DIM = 64
QK_ROPE_DIM = 32
QK_DIM = QK_NOPE_DIM + QK_ROPE_DIM
V_HEAD_DIM = 64
Q_LORA_RANK = 256
KV_LORA_RANK = 128
ROPE_THETA = 10000.0
LRU_WIDTH = 512
LRU_BLOCKS = 4
LRU_BLOCK_DIM = 128
CONV_WIDTH = 4
CONV_LEFT = 2
LRU_C = 8.0
NORM_EPS = 1e-6

LANES = 128
SUBLANES = 8
HEAD_PAD = LANES
VMEM_LIMIT_BYTES = 56 * 1024 * 1024

TM_IN = 512
TQ_ATTN = 256
TM_OUT = 256
N_SEG = SUBLANES
SEG_PAD = 8


def _rms(x, g):
    return x * lax.rsqrt(jnp.mean(x * x, axis=-1, keepdims=True) + NORM_EPS) * g


def _bf16(x):
    return x.astype(jnp.bfloat16)


def _dot(a, b):
    return jnp.dot(a, b, preferred_element_type=jnp.float32)


def _const_spec(shape):
    nd = len(shape)
    return pl.BlockSpec(shape, lambda *_: (0,) * nd, pipeline_mode=pl.Buffered(1))


def _rope_slot(t, c, s_lo, s_hi):
    return t * c + pltpu.roll(t, 16, 1) * s_hi + pltpu.roll(t, LANES - 16, 1) * s_lo


def _in_proj_kernel(x_ref, g1_ref, w_in_ref, gq_ref, wqb_ref, gkv_ref, wkvb_ref,
                    rc_ref, rlo_ref, rhi_ref,
                    q_ref, kt_ref, v_ref, lx_ref, gy_ref, gates_ref):
    h = _bf16(_rms(x_ref[...], g1_ref[...]))
    proj = _dot(h, w_in_ref[...])
    c0 = Q_LORA_RANK
    c1 = c0 + KV_LORA_RANK
    c2 = c1 + HEAD_PAD
    c3 = c2 + LRU_WIDTH
    c4 = c3 + LRU_WIDTH
    q_a = proj[:, :c0]
    kv_a = proj[:, c0:c1]
    k_rope = proj[:, c1:c2]
    lx_ref[...] = proj[:, c2:c3]
    gy_ref[...] = _bf16(jax.nn.gelu(proj[:, c3:c4]))
    gates_ref[...] = _bf16(jax.nn.sigmoid(proj[:, c4:]))

    rc = rc_ref[...]
    rlo = rlo_ref[...]
    rhi = rhi_ref[...]
    scale = QK_DIM ** -0.5
    q = _dot(_bf16(_rms(q_a, gq_ref[...])), wqb_ref[...])
    kv = _dot(_bf16(_rms(kv_a, gkv_ref[...])), wkvb_ref[...])
    k_rope = _rope_slot(k_rope, rc, rlo, rhi)
    for hd in range(N_HEADS):
        sl = slice(hd * HEAD_PAD, (hd + 1) * HEAD_PAD)
        q_ref[:, sl] = _bf16(_rope_slot(q[:, sl], rc, rlo, rhi) * scale)
        kt_ref[sl, :] = _bf16((kv[:, sl] + k_rope).T)
    v_ref[...] = _bf16(kv[:, N_HEADS * HEAD_PAD:])


def _in_proj(x2, g1, w_in_r, gq, wqb_r, gkv, wkvb_r, rc, rlo, rhi, batch, seq):
    t = x2.shape[0]
    d = x2.shape[1]
    tm = min(TM_IN, seq)
    per_seq = seq // tm
    row = lambda i: (i, 0)
    pos = lambda i: (i % per_seq, 0)
    bf = jnp.bfloat16
    return pl.pallas_call(
        _in_proj_kernel,
        grid=(t // tm,),
        in_specs=[
            pl.BlockSpec((tm, d), row),
            _const_spec(g1.shape), _const_spec(w_in_r.shape), _const_spec(gq.shape),
            _const_spec(wqb_r.shape), _const_spec(gkv.shape), _const_spec(wkvb_r.shape),
            pl.BlockSpec((tm, LANES), pos), pl.BlockSpec((tm, LANES), pos),
            pl.BlockSpec((tm, LANES), pos),
        ],
        out_specs=[
            pl.BlockSpec((tm, N_HEADS * HEAD_PAD), row),
            pl.BlockSpec((None, N_HEADS * HEAD_PAD, tm), lambda i: (i // per_seq, 0, i % per_seq)),
            pl.BlockSpec((tm, N_HEADS * V_HEAD_DIM), row),
            pl.BlockSpec((tm, LRU_WIDTH), row),
            pl.BlockSpec((tm, LRU_WIDTH), row),
            pl.BlockSpec((tm, 2 * d), row),
        ],
        out_shape=[
            jax.ShapeDtypeStruct((t, N_HEADS * HEAD_PAD), bf),
            jax.ShapeDtypeStruct((batch, N_HEADS * HEAD_PAD, seq), bf),
            jax.ShapeDtypeStruct((t, N_HEADS * V_HEAD_DIM), bf),
            jax.ShapeDtypeStruct((t, LRU_WIDTH), jnp.float32),
            jax.ShapeDtypeStruct((t, LRU_WIDTH), bf),
            jax.ShapeDtypeStruct((t, 2 * d), bf),
        ],
        compiler_params=pltpu.CompilerParams(
            dimension_semantics=("arbitrary",), vmem_limit_bytes=VMEM_LIMIT_BYTES),
        name="in_proj",
    )(x2, g1, w_in_r, gq, wqb_r, gkv, wkvb_r, rc, rlo, rhi)


def _shift_rows(x, down):
    rows = lax.broadcasted_iota(jnp.int32, x.shape, 0)
    if down:
        return jnp.where(rows == 0, 0.0, pltpu.roll(x, 1, 0))
    return jnp.where(rows == x.shape[0] - 1, 0.0, pltpu.roll(x, x.shape[0] - 1, 0))


def _rglru_kernel(lx_ref, gy_ref, cw_ref, cb_ref, wg_ref, bg_ref, lam_ref, o_ref,
                  xpad_ref, a_ref, u_ref, *, seq):
    seg = seq // N_SEG
    pitch = seg + SEG_PAD
    nb = LRU_BLOCKS
    f32 = jnp.float32

    zeros = jnp.zeros((SUBLANES, LRU_WIDTH), f32)
    xpad_ref[pl.ds(0, SUBLANES), :] = zeros
    xpad_ref[pl.ds(SUBLANES + seq, SUBLANES), :] = zeros
    xpad_ref[pl.ds(SUBLANES, seq), :] = lx_ref[...]

    lam = lam_ref[...]
    z = -lam
    softplus = jnp.maximum(z, 0.0) + jnp.log1p(jnp.exp(-jnp.abs(z)))
    decay = -LRU_C * softplus

    for s in range(N_SEG):
        base = SUBLANES + s * seg
        xc = cb_ref[...] + sum(
            xpad_ref[pl.ds(base + j - CONV_LEFT, seg), :] * cw_ref[pl.ds(j, 1), :]
            for j in range(CONV_WIDTH))
        for n in range(nb):
            ln = slice(n * LANES, (n + 1) * LANES)
            xn = xc[:, ln]
            g = _dot(_bf16(xn), wg_ref[n]) + bg_ref[n]
            for d in range(2):
                r = jax.nn.sigmoid(g[:, d * LANES:(d + 1) * LANES])
                i = jax.nn.sigmoid(g[:, (2 + d) * LANES:(3 + d) * LANES])
                log_a = r * decay[d:d + 1, ln]
                a = jnp.exp(log_a)
                mult = jnp.sqrt(-jnp.tanh(log_a) * (a * a + 1.0))
                a_ref[d * nb + n, pl.ds(s * pitch, seg), :] = a
                u_ref[d * nb + n, pl.ds(s * pitch, seg), :] = i * xn * mult

    def load(ref, k, j):
        return ref.at[k][pl.ds(j, N_SEG, stride=pitch), :]

    def sweep1(j, carry):
        out = []
        for k in range(2 * nb):
            h, p = carry[k]
            jj = j if k < nb else seg - 1 - j
            a = load(a_ref, k, jj)
            out.append((a * h + load(u_ref, k, jj), a * p))
        return tuple(out)

    zero = jnp.zeros((N_SEG, LANES), f32)
    one = jnp.ones((N_SEG, LANES), f32)
    ends = lax.fori_loop(0, seg, sweep1, tuple((zero, one) for _ in range(2 * nb)))

    starts = []
    for k in range(2 * nb):
        h_end, p_end = ends[k]
        st = zero
        for _ in range(N_SEG - 1):
            st = _shift_rows(h_end + p_end * st, down=(k < nb))
        starts.append(st)

    def sweep2(j, carry):
        out = []
        for k in range(2 * nb):
            jj = j if k < nb else seg - 1 - j
            h = load(a_ref, k, jj) * carry[k] + load(u_ref, k, jj)
            u_ref.at[k][pl.ds(jj, N_SEG, stride=pitch), :] = h
            out.append(h)
        return tuple(out)

    lax.fori_loop(0, seg, sweep2, tuple(starts))

    for s in range(N_SEG):
        hsum = jnp.concatenate(
            [u_ref[n, pl.ds(s * pitch, seg), :] + u_ref[nb + n, pl.ds(s * pitch, seg), :]
             for n in range(nb)], axis=1)
        o_ref[pl.ds(s * seg, seg), :] = _bf16(gy_ref[pl.ds(s * seg, seg), :].astype(f32) * hsum)


def _rglru(lx, gy, cw, cb, wg, bg, lam, batch, seq):
    t = lx.shape[0]
    pitch = seq // N_SEG + SEG_PAD
    row = lambda b: (b, 0)
    return pl.pallas_call(
        functools.partial(_rglru_kernel, seq=seq),
        grid=(batch,),
        in_specs=[
            pl.BlockSpec((seq, LRU_WIDTH), row), pl.BlockSpec((seq, LRU_WIDTH), row),
            _const_spec(cw.shape), _const_spec(cb.shape), _const_spec(wg.shape),
            _const_spec(bg.shape), _const_spec(lam.shape),
        ],
        out_specs=pl.BlockSpec((seq, LRU_WIDTH), row),
        out_shape=jax.ShapeDtypeStruct((t, LRU_WIDTH), jnp.bfloat16),
        scratch_shapes=[
            pltpu.VMEM((seq + 2 * SUBLANES, LRU_WIDTH), jnp.float32),
            pltpu.VMEM((2 * LRU_BLOCKS, N_SEG * pitch, LANES), jnp.float32),
            pltpu.VMEM((2 * LRU_BLOCKS, N_SEG * pitch, LANES), jnp.float32),
        ],
        compiler_params=pltpu.CompilerParams(
            dimension_semantics=("arbitrary",), vmem_limit_bytes=VMEM_LIMIT_BYTES),
        name="rglru",
    )(lx, gy, cw, cb, wg, bg, lam)


def _attn_kernel(q_ref, kt_ref, v_ref, o_ref):
    outs = []
    for hd in range(N_HEADS):
        sl = slice(hd * HEAD_PAD, (hd + 1) * HEAD_PAD)
        s = _dot(q_ref[:, sl], kt_ref[sl, :])
        p = jnp.exp(s - jnp.max(s, axis=-1, keepdims=True))
        denom = jnp.sum(p, axis=-1, keepdims=True)
        o = _dot(_bf16(p), v_ref[:, hd * V_HEAD_DIM:(hd + 1) * V_HEAD_DIM])
        outs.append(o / denom)
    o_ref[...] = _bf16(jnp.concatenate(outs, axis=1))


def _attention(q, kt, v, batch, seq):
    t = q.shape[0]
    tq = min(TQ_ATTN, seq)
    per_seq = seq // tq
    return pl.pallas_call(
        _attn_kernel,
        grid=(batch, per_seq),
        in_specs=[
            pl.BlockSpec((tq, N_HEADS * HEAD_PAD), lambda b, i: (b * per_seq + i, 0)),
            pl.BlockSpec((None, N_HEADS * HEAD_PAD, seq), lambda b, i: (b, 0, 0)),
            pl.BlockSpec((seq, N_HEADS * V_HEAD_DIM), lambda b, i: (b, 0)),
        ],
        out_specs=pl.BlockSpec((tq, N_HEADS * V_HEAD_DIM), lambda b, i: (b * per_seq + i, 0)),
        out_shape=jax.ShapeDtypeStruct((t, N_HEADS * V_HEAD_DIM), jnp.bfloat16),
        compiler_params=pltpu.CompilerParams(
            dimension_semantics=("arbitrary", "arbitrary"), vmem_limit_bytes=VMEM_LIMIT_BYTES),
        name="attention",
    )(q, kt, v)


def _merge_ffn_kernel(x_ref, attn_ref, rec_ref, gates_ref, woa_ref, wol_ref, wout_ref,
                      g2_ref, wg_ref, wu_ref, wd_ref, gf_ref, o_ref):
    d = x_ref.shape[1]
    a = _dot(attn_ref[...], woa_ref[...])
    r = _dot(rec_ref[...], wol_ref[...])
    gates = gates_ref[...].astype(jnp.float32)
    merged = gates[:, :d] * a + gates[:, d:] * r
    x1 = x_ref[...] + _dot(_bf16(merged), wout_ref[...])
    h2 = _bf16(_rms(x1, g2_ref[...]))
    act = jax.nn.silu(_dot(h2, wg_ref[...])) * _dot(h2, wu_ref[...])
    x2 = x1 + _dot(_bf16(act), wd_ref[...])
    o_ref[...] = _rms(x2, gf_ref[...])


def _merge_ffn(x2, attn, rec, gates, woa, wol, wout, g2, wg, wu, wd, gf):
    t, d = x2.shape
    tm = min(TM_OUT, t)
    row = lambda i: (i, 0)
    return pl.pallas_call(
        _merge_ffn_kernel,
        grid=(t // tm,),
        in_specs=[
            pl.BlockSpec((tm, d), row), pl.BlockSpec((tm, attn.shape[1]), row),
            pl.BlockSpec((tm, rec.shape[1]), row), pl.BlockSpec((tm, 2 * d), row),
            _const_spec(woa.shape), _const_spec(wol.shape), _const_spec(wout.shape),
            _const_spec(g2.shape), _const_spec(wg.shape), _const_spec(wu.shape),
            _const_spec(wd.shape), _const_spec(gf.shape),
        ],
        out_specs=pl.BlockSpec((tm, d), row),
        out_shape=jax.ShapeDtypeStruct((t, d), jnp.float32),
        compiler_params=pltpu.CompilerParams(
            dimension_semantics=("arbitrary",), vmem_limit_bytes=VMEM_LIMIT_BYTES),
        name="merge_ffn",
    )(x2, attn, rec, gates, woa, wol, wout, g2, wg, wu, wd, gf)


def _rope_slot_tables(seq):
    pos = jnp.arange(seq, dtype=jnp.float32)
    inv_freq = 1.0 / (ROPE_THETA ** (jnp.arange(0, QK_ROPE_DIM, 2, dtype=jnp.float32) / QK_ROPE_DIM))
    ang = pos[:, None] * inv_freq[None, :]
    cos, sin = jnp.cos(ang), jnp.sin(ang)
    half = QK_ROPE_DIM // 2
    ones = jnp.ones((seq, QK_NOPE_DIM), jnp.float32)
    z_half = jnp.zeros((seq, half), jnp.float32)
    z_nope = jnp.zeros((seq, QK_NOPE_DIM), jnp.float32)
    z_pad = jnp.zeros((seq, HEAD_PAD - QK_DIM), jnp.float32)
    rc = jnp.concatenate([ones, cos, cos, z_pad], axis=1)
    rlo = jnp.concatenate([z_nope, -sin, z_half, z_pad], axis=1)
    rhi = jnp.concatenate([z_nope, z_half, sin, z_pad], axis=1)
    return rc, rlo, rhi


def _head_slots(w, width):
    k = w.shape[0]
    w = w.reshape(k, N_HEADS, width)
    w = jnp.pad(w, ((0, 0), (0, 0), (0, HEAD_PAD - width)))
    return w.reshape(k, N_HEADS * HEAD_PAD)


def kernel(x, norm1_g, w_in, q_a_norm_g, w_q_b, kv_a_norm_g, w_kv_b, w_o_attn,
           conv_w, conv_b, w_rgate, b_rgate, w_igate, b_igate, lru_lambda, w_o_lru,
           w_out, norm2_g, w_ffn_gate, w_ffn_up, w_ffn_down, final_g):
    batch, seq, d = x.shape
    depth = w_in.shape[0]
    bf = jnp.bfloat16
    c_qa = Q_LORA_RANK
    c_kva = c_qa + KV_LORA_RANK
    c_kr = c_kva + QK_ROPE_DIM
    rc, rlo, rhi = _rope_slot_tables(seq)
    assert depth == 1, "the final rmsnorm is fused into the single layer's last kernel"
    l = 0
    x2 = x.reshape(batch * seq, d)

    wi = w_in[l]
    w_kr = jnp.pad(wi[:, c_kva:c_kr], ((0, 0), (QK_NOPE_DIM, HEAD_PAD - QK_DIM)))
    w_in_r = jnp.concatenate([wi[:, :c_kva], w_kr, wi[:, c_kr:]], axis=1).astype(bf)
    wqb_r = _head_slots(w_q_b[l], QK_DIM).astype(bf)
    wkv = w_kv_b[l].reshape(KV_LORA_RANK, N_HEADS, QK_NOPE_DIM + V_HEAD_DIM)
    wk_r = _head_slots(wkv[:, :, :QK_NOPE_DIM].reshape(KV_LORA_RANK, -1), QK_NOPE_DIM)
    wv = wkv[:, :, QK_NOPE_DIM:].reshape(KV_LORA_RANK, -1)
    wkvb_r = jnp.concatenate([wk_r, wv], axis=1).astype(bf)

    q, kt, v, lx, gy, gates = _in_proj(
        x2, norm1_g[l][None], w_in_r, q_a_norm_g[l][None], wqb_r, kv_a_norm_g[l][None],
        wkvb_r, rc, rlo, rhi, batch, seq)

    wg = jnp.concatenate([w_rgate[l, 0], w_rgate[l, 1], w_igate[l, 0], w_igate[l, 1]], axis=2).astype(bf)
    bg = jnp.concatenate([b_rgate[l, 0], b_rgate[l, 1], b_igate[l, 0], b_igate[l, 1]], axis=1)[:, None, :]
    rec = _rglru(lx, gy, conv_w[l], conv_b[l][None], wg, bg, lru_lambda[l], batch, seq)

    attn = _attention(q, kt, v, batch, seq)

    out = _merge_ffn(x2, attn, rec, gates, w_o_attn[l].astype(bf), w_o_lru[l].astype(bf),
                     w_out[l].astype(bf), norm2_g[l][None], w_ffn_gate[l].astype(bf),
                     w_ffn_up[l].astype(bf), w_ffn_down[l].astype(bf), final_g[None])
    return out.reshape(batch, seq, d)
```

```python
import functools

import jax
import jax.numpy as jnp
from jax import lax
from jax.experimental import pallas as pl
from jax.experimental.pallas import tpu as pltpu

N_HEADS = 8
QK_NOPE_DIM = 64
QK_ROPE_DIM = 32
QK_DIM = QK_NOPE_DIM + QK_ROPE_DIM
V_HEAD_DIM = 64
Q_LORA_RANK = 256
KV_LORA_RANK = 128
ROPE_THETA = 10000.0
LRU_WIDTH = 512
LRU_BLOCKS = 4
LRU_BLOCK_DIM = 128
CONV_WIDTH = 4
CONV_LEFT = 2
LRU_C = 8.0
NORM_EPS = 1e-6
LOG2_E = 1.4426950408889634

LANES = 128
SUBLANES = 8
HEAD_PAD = LANES
VMEM_LIMIT_BYTES = 56 * 1024 * 1024

TM_IN = 512
TQ_ATTN = 512
TK_ATTN = 256
TM_OUT = 256
N_SEG = SUBLANES
SEG_PAD = 4
SWEEP_UNROLL = 4
SEG_OFF = SUBLANES
SCAN_CHUNK = 32


def _sigmoid(x):
    return 0.5 * jnp.tanh(0.5 * x) + 0.5


def _rms(x, g):
    return x * lax.rsqrt(jnp.mean(x * x, axis=-1, keepdims=True) + NORM_EPS) * g


def _bf16(x):
    return x.astype(jnp.bfloat16)


def _dot(a, b):
    return jnp.dot(a, b, preferred_element_type=jnp.float32)


def _const_spec(shape):
    nd = len(shape)
    return pl.BlockSpec(shape, lambda *_: (0,) * nd, pipeline_mode=pl.Buffered(1))


def _rope_slot(t, c, s_lo, s_hi):
    return t * c + pltpu.roll(t, 16, 1) * s_hi + pltpu.roll(t, LANES - 16, 1) * s_lo


def _in_proj_kernel(x_ref, g1_ref, w_in_ref, gq_ref, wqb_ref, gkv_ref, wkvb_ref,
                    rc_ref, rlo_ref, rhi_ref,
                    qt_ref, k_ref, vt_ref, lx_ref, gy_ref, gates_ref):
    h = _bf16(_rms(x_ref[...], g1_ref[...]))
    proj = _dot(h, w_in_ref[...])
    c0 = Q_LORA_RANK
    c1 = c0 + KV_LORA_RANK
    c2 = c1 + HEAD_PAD
    c3 = c2 + LRU_WIDTH
    c4 = c3 + LRU_WIDTH
    q_a = proj[:, :c0]
    kv_a = proj[:, c0:c1]
    k_rope = proj[:, c1:c2]
    lx_ref[...] = proj[:, c2:c3]
    gy_ref[...] = _bf16(jax.nn.gelu(proj[:, c3:c4]))
    gates_ref[...] = _bf16(jax.nn.sigmoid(proj[:, c4:]))

    rc = rc_ref[...]
    rlo = rlo_ref[...]
    rhi = rhi_ref[...]
    scale = QK_DIM ** -0.5 * LOG2_E
    q = _dot(_bf16(_rms(q_a, gq_ref[...])), wqb_ref[...])
    kv = _dot(_bf16(_rms(kv_a, gkv_ref[...])), wkvb_ref[...])
    k_rope = _rope_slot(k_rope, rc, rlo, rhi)
    ones_lane = (lax.broadcasted_iota(jnp.int32, (1, HEAD_PAD), 1) == V_HEAD_DIM).astype(jnp.float32)
    for hd in range(N_HEADS):
        sl = slice(hd * HEAD_PAD, (hd + 1) * HEAD_PAD)
        vsl = slice((N_HEADS + hd) * HEAD_PAD, (N_HEADS + hd + 1) * HEAD_PAD)
        qt_ref[sl, :] = _bf16((_rope_slot(q[:, sl], rc, rlo, rhi) * scale).T)
        k_ref[:, sl] = _bf16(kv[:, sl] + k_rope)
        vt_ref[sl, :] = _bf16((kv[:, vsl] + ones_lane).T)


def _in_proj(x2, g1, w_in_r, gq, wqb_r, gkv, wkvb_r, rc, rlo, rhi, batch, seq):
    t = x2.shape[0]
    d = x2.shape[1]
    tm = min(TM_IN, seq)
    per_seq = seq // tm
    row = lambda i: (i, 0)
    pos = lambda i: (i % per_seq, 0)
    col = lambda i: (i // per_seq, 0, i % per_seq)
    bf = jnp.bfloat16
    return pl.pallas_call(
        _in_proj_kernel,
        grid=(t // tm,),
        in_specs=[
            pl.BlockSpec((tm, d), row),
            _const_spec(g1.shape), _const_spec(w_in_r.shape), _const_spec(gq.shape),
            _const_spec(wqb_r.shape), _const_spec(gkv.shape), _const_spec(wkvb_r.shape),
            pl.BlockSpec((tm, LANES), pos), pl.BlockSpec((tm, LANES), pos),
            pl.BlockSpec((tm, LANES), pos),
        ],
        out_specs=[
            pl.BlockSpec((None, N_HEADS * HEAD_PAD, tm), col),
            pl.BlockSpec((tm, N_HEADS * HEAD_PAD), row),
            pl.BlockSpec((None, N_HEADS * HEAD_PAD, tm), col),
            pl.BlockSpec((tm, LRU_WIDTH), row),
            pl.BlockSpec((tm, LRU_WIDTH), row),
            pl.BlockSpec((tm, 2 * d), row),
        ],
        out_shape=[
            jax.ShapeDtypeStruct((batch, N_HEADS * HEAD_PAD, seq), bf),
            jax.ShapeDtypeStruct((t, N_HEADS * HEAD_PAD), bf),
            jax.ShapeDtypeStruct((batch, N_HEADS * HEAD_PAD, seq), bf),
            jax.ShapeDtypeStruct((t, LRU_WIDTH), jnp.float32),
            jax.ShapeDtypeStruct((t, LRU_WIDTH), bf),
            jax.ShapeDtypeStruct((t, 2 * d), bf),
        ],
        compiler_params=pltpu.CompilerParams(
            dimension_semantics=("arbitrary",), vmem_limit_bytes=VMEM_LIMIT_BYTES),
        name="in_proj",
    )(x2, g1, w_in_r, gq, wqb_r, gkv, wkvb_r, rc, rlo, rhi)


def _shift_rows(x, down):
    rows = lax.broadcasted_iota(jnp.int32, x.shape, 0)
    if down:
        return jnp.where(rows == 0, 0.0, pltpu.roll(x, 1, 0))
    return jnp.where(rows == x.shape[0] - 1, 0.0, pltpu.roll(x, x.shape[0] - 1, 0))


def _rglru_kernel(lx_ref, gy_ref, cw_ref, cb_ref, wg_ref, bg_ref, lam_ref, o_ref,
                  x_ref, a_ref, u_ref, h_ref, *, seq):
    seg = seq // N_SEG
    pitch = seg + SEG_PAD
    nb = LRU_BLOCKS
    jc = min(SCAN_CHUNK, seg)
    f32 = jnp.float32

    zeros = jnp.zeros((SUBLANES, LANES), f32)
    for n in range(nb):
        ln = slice(n * LANES, (n + 1) * LANES)
        x_ref[n, pl.ds(0, SUBLANES), :] = zeros
        x_ref[n, pl.ds(SEG_OFF + N_SEG * pitch - SEG_PAD, SUBLANES), :] = zeros
        for s in range(N_SEG):
            x_ref[n, pl.ds(SEG_OFF + s * pitch, seg), :] = lx_ref[pl.ds(s * seg, seg), ln]
        for s in range(1, N_SEG):
            x_ref[n, pl.ds(SEG_OFF + s * pitch - CONV_LEFT, CONV_LEFT), :] = (
                lx_ref[pl.ds(s * seg - CONV_LEFT, CONV_LEFT), ln])
            x_ref[n, pl.ds(SEG_OFF + (s - 1) * pitch + seg, 1), :] = lx_ref[pl.ds(s * seg, 1), ln]

    lam = lam_ref[...]
    z = -lam
    softplus = jnp.maximum(z, 0.0) + jnp.log1p(jnp.exp(-jnp.abs(z)))
    half_decay = (-0.5 * LRU_C) * softplus
    cw = cw_ref[...]
    cb = cb_ref[...]

    def gates(c, _):
        j0 = c * jc
        for n in range(nb):
            ln = slice(n * LANES, (n + 1) * LANES)
            xv = [x_ref.at[n][pl.ds(SEG_OFF - CONV_LEFT + j0 + r, N_SEG, stride=pitch), :]
                  for r in range(jc + CONV_WIDTH - 1)]
            xc = cb[:, ln] + sum(
                jnp.concatenate(xv[o:o + jc], axis=0) * cw[o:o + 1, ln] for o in range(CONV_WIDTH))
            t = jnp.tanh(_dot(_bf16(xc), wg_ref[n]) + bg_ref[n])
            hx = 0.5 * xc
            rows = pl.ds(pl.multiple_of(j0 * N_SEG, jc * N_SEG), jc * N_SEG)
            for d in range(2):
                hd = half_decay[d:d + 1, ln]
                log_a = t[:, d * LANES:(d + 1) * LANES] * hd + hd
                a = jnp.exp(log_a)
                m2 = 1.0 - a * a
                mult = jnp.where(m2 > 0.0, m2 * lax.rsqrt(m2), 0.0)
                a_ref[d * nb + n, rows, :] = a
                u_ref[d * nb + n, rows, :] = (t[:, (2 + d) * LANES:(3 + d) * LANES] + 1.0) * (hx * mult)
        return 0

    lax.fori_loop(0, seg // jc, gates, 0)

    def row(j):
        return (pl.ds(pl.multiple_of(j * N_SEG, N_SEG), N_SEG), slice(None))

    def sweep1(j, carry):
        out = []
        for k in range(2 * nb):
            h, p = carry[k]
            jj = j if k < nb else seg - 1 - j
            a = a_ref.at[k][row(jj)]
            out.append((a * h + u_ref.at[k][row(jj)], a * p))
        return tuple(out)

    zero = jnp.zeros((N_SEG, LANES), f32)
    one = jnp.ones((N_SEG, LANES), f32)
    ends = lax.fori_loop(0, seg, sweep1, tuple((zero, one) for _ in range(2 * nb)),
                         unroll=SWEEP_UNROLL)

    starts = []
    for k in range(2 * nb):
        h_end, p_end = ends[k]
        st = zero
        for _ in range(N_SEG - 1):
            st = _shift_rows(h_end + p_end * st, down=(k < nb))
        starts.append(st)

    def sweep2(j, carry):
        out = []
        for k in range(2 * nb):
            jj = j if k < nb else seg - 1 - j
            h = a_ref.at[k][row(jj)] * carry[k] + u_ref.at[k][row(jj)]
            h_ref.at[k][row(jj)] = h
            out.append(h)
        return tuple(out)

    lax.fori_loop(0, seg, sweep2, tuple(starts), unroll=SWEEP_UNROLL)

    def unscan(j, _):
        for n in range(nb):
            hsum = h_ref.at[n][row(j)] + h_ref.at[nb + n][row(j)]
            x_ref.at[n][pl.ds(SEG_OFF + j, N_SEG, stride=pitch), :] = hsum
        return 0

    lax.fori_loop(0, seg, unscan, 0, unroll=SWEEP_UNROLL)

    for s in range(N_SEG):
        hsum = jnp.concatenate(
            [x_ref[n, pl.ds(SEG_OFF + s * pitch, seg), :] for n in range(nb)], axis=1)
        o_ref[pl.ds(s * seg, seg), :] = _bf16(gy_ref[pl.ds(s * seg, seg), :].astype(f32) * hsum)


def _rglru(lx, gy, cw, cb, wg, bg, lam, batch, seq):
    t = lx.shape[0]
    pitch = seq // N_SEG + SEG_PAD
    row = lambda b: (b, 0)
    scan_buf = pltpu.VMEM((2 * LRU_BLOCKS, seq, LANES), jnp.float32)
    return pl.pallas_call(
        functools.partial(_rglru_kernel, seq=seq),
        grid=(batch,),
        in_specs=[
            pl.BlockSpec((seq, LRU_WIDTH), row), pl.BlockSpec((seq, LRU_WIDTH), row),
            _const_spec(cw.shape), _const_spec(cb.shape), _const_spec(wg.shape),
            _const_spec(bg.shape), _const_spec(lam.shape),
        ],
        out_specs=pl.BlockSpec((seq, LRU_WIDTH), row),
        out_shape=jax.ShapeDtypeStruct((t, LRU_WIDTH), jnp.bfloat16),
        scratch_shapes=[
            pltpu.VMEM((LRU_BLOCKS, SEG_OFF + N_SEG * pitch + SUBLANES, LANES), jnp.float32),
            scan_buf, scan_buf, scan_buf,
        ],
        compiler_params=pltpu.CompilerParams(
            dimension_semantics=("arbitrary",), vmem_limit_bytes=VMEM_LIMIT_BYTES),
        name="rglru",
    )(lx, gy, cw, cb, wg, bg, lam)


def _attn_kernel(qt_ref, k_ref, vt_ref, o_ref):
    def scores(hd):
        sl = slice(hd * HEAD_PAD, (hd + 1) * HEAD_PAD)
        return _dot(k_ref[:, sl], qt_ref[sl, :])

    def values(hd, pt):
        sl = slice(hd * HEAD_PAD, (hd + 1) * HEAD_PAD)
        ot = _dot(vt_ref[sl, :], pt)
        return ot[:V_HEAD_DIM] / ot[V_HEAD_DIM:V_HEAD_DIM + 1]

    outs = []
    st = scores(0)
    pt_prev = None
    for hd in range(N_HEADS):
        st_next = scores(hd + 1) if hd + 1 < N_HEADS else None
        if pt_prev is not None:
            outs.append(values(hd - 1, pt_prev))
        pt_prev = _bf16(jnp.exp2(st - jnp.max(st, axis=0, keepdims=True)))
        st = st_next
    outs.append(values(N_HEADS - 1, pt_prev))
    o_ref[...] = _bf16(jnp.concatenate(outs, axis=0).T)


def _attention(qt, k, vt, batch, seq):
    t = k.shape[0]
    tq = min(TQ_ATTN, seq)
    per_seq = seq // tq
    return pl.pallas_call(
        _attn_kernel,
        grid=(batch, per_seq),
        in_specs=[
            pl.BlockSpec((None, N_HEADS * HEAD_PAD, tq), lambda b, i: (b, 0, i)),
            pl.BlockSpec((seq, N_HEADS * HEAD_PAD), lambda b, i: (b, 0)),
            pl.BlockSpec((None, N_HEADS * HEAD_PAD, seq), lambda b, i: (b, 0, 0)),
        ],
        out_specs=pl.BlockSpec((tq, N_HEADS * V_HEAD_DIM), lambda b, i: (b * per_seq + i, 0)),
        out_shape=jax.ShapeDtypeStruct((t, N_HEADS * V_HEAD_DIM), jnp.bfloat16),
        compiler_params=pltpu.CompilerParams(
            dimension_semantics=("arbitrary", "arbitrary"), vmem_limit_bytes=VMEM_LIMIT_BYTES),
        name="attention",
    )(qt, k, vt)


def _merge_ffn_kernel(x_ref, attn_ref, rec_ref, gates_ref, woa_ref, wol_ref, wout_ref,
                      g2_ref, wg_ref, wu_ref, wd_ref, gf_ref, o_ref):
    d = x_ref.shape[1]
    a = _dot(attn_ref[...], woa_ref[...])
    r = _dot(rec_ref[...], wol_ref[...])
    gates = gates_ref[...].astype(jnp.float32)
    merged = gates[:, :d] * a + gates[:, d:] * r
    x1 = x_ref[...] + _dot(_bf16(merged), wout_ref[...])
    h2 = _bf16(_rms(x1, g2_ref[...]))
    act = jax.nn.silu(_dot(h2, wg_ref[...])) * _dot(h2, wu_ref[...])
    x2 = x1 + _dot(_bf16(act), wd_ref[...])
    o_ref[...] = _rms(x2, gf_ref[...])


def _merge_ffn(x2, attn, rec, gates, woa, wol, wout, g2, wg, wu, wd, gf):
    t, d = x2.shape
    tm = min(TM_OUT, t)
    row = lambda i: (i, 0)
    return pl.pallas_call(
        _merge_ffn_kernel,
        grid=(t // tm,),
        in_specs=[
            pl.BlockSpec((tm, d), row), pl.BlockSpec((tm, attn.shape[1]), row),
            pl.BlockSpec((tm, rec.shape[1]), row), pl.BlockSpec((tm, 2 * d), row),
            _const_spec(woa.shape), _const_spec(wol.shape), _const_spec(wout.shape),
            _const_spec(g2.shape), _const_spec(wg.shape), _const_spec(wu.shape),
            _const_spec(wd.shape), _const_spec(gf.shape),
        ],
        out_specs=pl.BlockSpec((tm, d), row),
        out_shape=jax.ShapeDtypeStruct((t, d), jnp.float32),
        compiler_params=pltpu.CompilerParams(
            dimension_semantics=("arbitrary",), vmem_limit_bytes=VMEM_LIMIT_BYTES),
        name="merge_ffn",
    )(x2, attn, rec, gates, woa, wol, wout, g2, wg, wu, wd, gf)


def _rope_slot_tables(seq):
    pos = jnp.arange(seq, dtype=jnp.float32)
    inv_freq = 1.0 / (ROPE_THETA ** (jnp.arange(0, QK_ROPE_DIM, 2, dtype=jnp.float32) / QK_ROPE_DIM))
    ang = pos[:, None] * inv_freq[None, :]
    cos, sin = jnp.cos(ang), jnp.sin(ang)
    half = QK_ROPE_DIM // 2
    ones = jnp.ones((seq, QK_NOPE_DIM), jnp.float32)
    z_half = jnp.zeros((seq, half), jnp.float32)
    z_nope = jnp.zeros((seq, QK_NOPE_DIM), jnp.float32)
    z_pad = jnp.zeros((seq, HEAD_PAD - QK_DIM), jnp.float32)
    rc = jnp.concatenate([ones, cos, cos, z_pad], axis=1)
    rlo = jnp.concatenate([z_nope, -sin, z_half, z_pad], axis=1)
    rhi = jnp.concatenate([z_nope, z_half, sin, z_pad], axis=1)
    return rc, rlo, rhi


def _head_slots(w, width):
    k = w.shape[0]
    w = w.reshape(k, N_HEADS, width)
    w = jnp.pad(w, ((0, 0), (0, 0), (0, HEAD_PAD - width)))
    return w.reshape(k, N_HEADS * HEAD_PAD)


def kernel(x, norm1_g, w_in, q_a_norm_g, w_q_b, kv_a_norm_g, w_kv_b, w_o_attn,
           conv_w, conv_b, w_rgate, b_rgate, w_igate, b_igate, lru_lambda, w_o_lru,
           w_out, norm2_g, w_ffn_gate, w_ffn_up, w_ffn_down, final_g):
    batch, seq, d = x.shape
    depth = w_in.shape[0]
    bf = jnp.bfloat16
    c_qa = Q_LORA_RANK
    c_kva = c_qa + KV_LORA_RANK
    c_kr = c_kva + QK_ROPE_DIM
    rc, rlo, rhi = _rope_slot_tables(seq)
    assert depth == 1, "the final rmsnorm is fused into the single layer's last kernel"
    l = 0
    x2 = x.reshape(batch * seq, d)

    wi = w_in[l]
    w_kr = jnp.pad(wi[:, c_kva:c_kr], ((0, 0), (QK_NOPE_DIM, HEAD_PAD - QK_DIM)))
    w_in_r = jnp.concatenate([wi[:, :c_kva], w_kr, wi[:, c_kr:]], axis=1).astype(bf)
    wqb_r = _head_slots(w_q_b[l], QK_DIM).astype(bf)
    wkv = w_kv_b[l].reshape(KV_LORA_RANK, N_HEADS, QK_NOPE_DIM + V_HEAD_DIM)
    wk_r = _head_slots(wkv[:, :, :QK_NOPE_DIM].reshape(KV_LORA_RANK, -1), QK_NOPE_DIM)
    wv_r = _head_slots(wkv[:, :, QK_NOPE_DIM:].reshape(KV_LORA_RANK, -1), V_HEAD_DIM)
    wkvb_r = jnp.concatenate([wk_r, wv_r], axis=1).astype(bf)

    qt, k, vt, lx, gy, gates = _in_proj(
        x2, norm1_g[l][None], w_in_r, q_a_norm_g[l][None], wqb_r, kv_a_norm_g[l][None],
        wkvb_r, rc, rlo, rhi, batch, seq)

    wg = (0.5 * jnp.concatenate([w_rgate[l, 0], w_rgate[l, 1], w_igate[l, 0], w_igate[l, 1]], axis=2)).astype(bf)
    bg = 0.5 * jnp.concatenate([b_rgate[l, 0], b_rgate[l, 1], b_igate[l, 0], b_igate[l, 1]], axis=1)[:, None, :]
    rec = _rglru(lx, gy, conv_w[l], conv_b[l][None], wg, bg, lru_lambda[l], batch, seq)

    attn = _attention(qt, k, vt, batch, seq)

    out = _merge_ffn(x2, attn, rec, gates, w_o_attn[l].astype(bf), w_o_lru[l].astype(bf),
                     w_out[l].astype(bf), norm2_g[l][None], w_ffn_gate[l].astype(bf),
                     w_ffn_up[l].astype(bf), w_ffn_down[l].astype(bf), final_g[None])
    return out.reshape(batch, seq, d)
```

```python
import functools

import jax
import jax.numpy as jnp
from jax import lax
from jax.experimental import pallas as pl
from jax.experimental.pallas import tpu as pltpu

N_HEADS = 8
QK_NOPE_DIM = 64
QK_ROPE_DIM = 32
QK_DIM = QK_NOPE_DIM + QK_ROPE_DIM
V_HEAD_DIM = 64
Q_LORA_RANK = 256
KV_LORA_RANK = 128
ROPE_THETA = 10000.0
LRU_WIDTH = 512
LRU_BLOCKS = 4
LRU_BLOCK_DIM = 128
CONV_WIDTH = 4
CONV_LEFT = 2
LRU_C = 8.0
NORM_EPS = 1e-6
LOG2_E = 1.4426950408889634

LANES = 128
SUBLANES = 8
HEAD_PAD = LANES
VMEM_LIMIT_BYTES = 56 * 1024 * 1024

TM_IN = 512
TQ_ATTN = 512
TM_OUT = 512
N_SEG = SUBLANES
SEG_PAD = 4
SWEEP_UNROLL = 4
SEG_OFF = SUBLANES
SCAN_CHUNK = 32


def _sigmoid(x):
    return 0.5 * jnp.tanh(0.5 * x) + 0.5


def _rms(x, g):
    return x * lax.rsqrt(jnp.mean(x * x, axis=-1, keepdims=True) + NORM_EPS) * g


def _bf16(x):
    return x.astype(jnp.bfloat16)


def _dot(a, b):
    return jnp.dot(a, b, preferred_element_type=jnp.float32)


def _const_spec(shape):
    nd = len(shape)
    return pl.BlockSpec(shape, lambda *_: (0,) * nd, pipeline_mode=pl.Buffered(1))


def _rope_slot(t, c, s_lo, s_hi):
    return t * c + pltpu.roll(t, 16, 1) * s_hi + pltpu.roll(t, LANES - 16, 1) * s_lo


def _in_proj_kernel(x_ref, g1_ref, w_in_ref, gq_ref, wqb_ref, gkv_ref, wkvb_ref,
                    rc_ref, rlo_ref, rhi_ref,
                    qt_ref, k_ref, vt_ref, lx_ref, gy_ref, gates_ref, lat_ref):
    c0 = Q_LORA_RANK
    c1 = c0 + KV_LORA_RANK
    c2 = c1 + HEAD_PAD
    c3 = c2 + LRU_WIDTH
    c4 = c3 + LRU_WIDTH

    @pl.when(pl.program_id(0) == 0)
    def _():
        lat_ref[...] = jnp.zeros(lat_ref.shape, lat_ref.dtype)

    q_a = lat_ref[:, :c0]
    kv_a = lat_ref[:, c0:c1]
    k_rope = lat_ref[:, c1:c2]
    rc = rc_ref[...]
    rlo = rlo_ref[...]
    rhi = rhi_ref[...]
    scale = QK_DIM ** -0.5 * LOG2_E
    q = _dot(_bf16(_rms(q_a, gq_ref[...])), wqb_ref[...])
    kv = _dot(_bf16(_rms(kv_a, gkv_ref[...])), wkvb_ref[...])
    k_rope = _rope_slot(k_rope, rc, rlo, rhi)
    ones_lane = (lax.broadcasted_iota(jnp.int32, (1, HEAD_PAD), 1) == V_HEAD_DIM).astype(jnp.float32)
    for hd in range(N_HEADS):
        sl = slice(hd * HEAD_PAD, (hd + 1) * HEAD_PAD)
        vsl = slice((N_HEADS + hd) * HEAD_PAD, (N_HEADS + hd + 1) * HEAD_PAD)
        qt_ref[sl, :] = _bf16((_rope_slot(q[:, sl], rc, rlo, rhi) * scale).T)
        k_ref[:, sl] = _bf16(kv[:, sl] + k_rope)
        vt_ref[sl, :] = _bf16((kv[:, vsl] + ones_lane).T)

    h = _bf16(_rms(x_ref[...], g1_ref[...]))
    proj = _dot(h, w_in_ref[...])
    lat_ref[...] = proj[:, :c2]
    lx_ref[...] = proj[:, c2:c3]
    gy_ref[...] = _bf16(jax.nn.gelu(proj[:, c3:c4]))
    gates_ref[...] = _bf16(jax.nn.sigmoid(proj[:, c4:]))


def _in_proj(x2, g1, w_in_r, gq, wqb_r, gkv, wkvb_r, rc, rlo, rhi, batch, seq):
    t = x2.shape[0]
    d = x2.shape[1]
    tm = min(TM_IN, seq)
    per_seq = seq // tm
    n = t // tm
    cur = lambda i: jnp.minimum(i, n - 1)
    prev = lambda i: jnp.maximum(i - 1, 0)
    row = lambda i: (cur(i), 0)
    prow = lambda i: (prev(i), 0)
    ppos = lambda i: (prev(i) % per_seq, 0)
    pcol = lambda i: (prev(i) // per_seq, 0, prev(i) % per_seq)
    bf = jnp.bfloat16
    return pl.pallas_call(
        _in_proj_kernel,
        grid=(n + 1,),
        in_specs=[
            pl.BlockSpec((tm, d), row),
            _const_spec(g1.shape), _const_spec(w_in_r.shape), _const_spec(gq.shape),
            _const_spec(wqb_r.shape), _const_spec(gkv.shape), _const_spec(wkvb_r.shape),
            pl.BlockSpec((tm, LANES), ppos), pl.BlockSpec((tm, LANES), ppos),
            pl.BlockSpec((tm, LANES), ppos),
        ],
        out_specs=[
            pl.BlockSpec((None, N_HEADS * HEAD_PAD, tm), pcol),
            pl.BlockSpec((tm, N_HEADS * HEAD_PAD), prow),
            pl.BlockSpec((None, N_HEADS * HEAD_PAD, tm), pcol),
            pl.BlockSpec((tm, LRU_WIDTH), row),
            pl.BlockSpec((tm, LRU_WIDTH), row),
            pl.BlockSpec((tm, 2 * d), row),
        ],
        out_shape=[
            jax.ShapeDtypeStruct((batch, N_HEADS * HEAD_PAD, seq), bf),
            jax.ShapeDtypeStruct((t, N_HEADS * HEAD_PAD), bf),
            jax.ShapeDtypeStruct((batch, N_HEADS * HEAD_PAD, seq), bf),
            jax.ShapeDtypeStruct((t, LRU_WIDTH), jnp.float32),
            jax.ShapeDtypeStruct((t, LRU_WIDTH), bf),
            jax.ShapeDtypeStruct((t, 2 * d), bf),
        ],
        scratch_shapes=[pltpu.VMEM((tm, Q_LORA_RANK + KV_LORA_RANK + HEAD_PAD), jnp.float32)],
        compiler_params=pltpu.CompilerParams(
            dimension_semantics=("arbitrary",), vmem_limit_bytes=VMEM_LIMIT_BYTES),
        name="in_proj",
    )(x2, g1, w_in_r, gq, wqb_r, gkv, wkvb_r, rc, rlo, rhi)


def _shift_rows(x, down):
    rows = lax.broadcasted_iota(jnp.int32, x.shape, 0)
    if down:
        return jnp.where(rows == 0, 0.0, pltpu.roll(x, 1, 0))
    return jnp.where(rows == x.shape[0] - 1, 0.0, pltpu.roll(x, x.shape[0] - 1, 0))


def _rglru_kernel(lx_ref, gy_ref, cw_ref, cb_ref, wg_ref, bg_ref, lam_ref, o_ref,
                  x_ref, a_ref, u_ref, h_ref, *, seq):
    seg = seq // N_SEG
    pitch = seg + SEG_PAD
    nb = LRU_BLOCKS
    jc = min(SCAN_CHUNK, seg)
    f32 = jnp.float32

    zeros = jnp.zeros((SUBLANES, LANES), f32)
    for n in range(nb):
        ln = slice(n * LANES, (n + 1) * LANES)
        x_ref[n, pl.ds(0, SUBLANES), :] = zeros
        x_ref[n, pl.ds(SEG_OFF + N_SEG * pitch - SEG_PAD, SUBLANES), :] = zeros
        for s in range(N_SEG):
            x_ref[n, pl.ds(SEG_OFF + s * pitch, seg), :] = lx_ref[pl.ds(s * seg, seg), ln]
        for s in range(1, N_SEG):
            x_ref[n, pl.ds(SEG_OFF + s * pitch - CONV_LEFT, CONV_LEFT), :] = (
                lx_ref[pl.ds(s * seg - CONV_LEFT, CONV_LEFT), ln])
            x_ref[n, pl.ds(SEG_OFF + (s - 1) * pitch + seg, 1), :] = lx_ref[pl.ds(s * seg, 1), ln]

    lam = lam_ref[...]
    z = -lam
    softplus = jnp.maximum(z, 0.0) + jnp.log1p(jnp.exp(-jnp.abs(z)))
    half_decay = (-0.5 * LRU_C) * softplus
    cw = cw_ref[...]
    cb = cb_ref[...]

    def gates(c, _):
        j0 = c * jc
        for n in range(nb):
            ln = slice(n * LANES, (n + 1) * LANES)
            xv = [x_ref.at[n][pl.ds(SEG_OFF - CONV_LEFT + j0 + r, N_SEG, stride=pitch), :]
                  for r in range(jc + CONV_WIDTH - 1)]
            xc = cb[:, ln] + sum(
                jnp.concatenate(xv[o:o + jc], axis=0) * cw[o:o + 1, ln] for o in range(CONV_WIDTH))
            t = jnp.tanh(_dot(_bf16(xc), wg_ref[n]) + bg_ref[n])
            hx = 0.5 * xc
            rows = pl.ds(pl.multiple_of(j0 * N_SEG, jc * N_SEG), jc * N_SEG)
            for d in range(2):
                hd = half_decay[d:d + 1, ln]
                log_a = t[:, d * LANES:(d + 1) * LANES] * hd + hd
                a = jnp.exp(log_a)
                m2 = 1.0 - a * a
                mult = jnp.where(m2 > 0.0, m2 * lax.rsqrt(m2), 0.0)
                a_ref[d * nb + n, rows, :] = a
                u_ref[d * nb + n, rows, :] = (t[:, (2 + d) * LANES:(3 + d) * LANES] + 1.0) * (hx * mult)
        return 0

    lax.fori_loop(0, seg // jc, gates, 0)

    def row(j):
        return (pl.ds(pl.multiple_of(j * N_SEG, N_SEG), N_SEG), slice(None))

    def sweep1(j, carry):
        out = []
        for k in range(2 * nb):
            h, p = carry[k]
            jj = j if k < nb else seg - 1 - j
            a = a_ref.at[k][row(jj)]
            out.append((a * h + u_ref.at[k][row(jj)], a * p))
        return tuple(out)

    zero = jnp.zeros((N_SEG, LANES), f32)
    one = jnp.ones((N_SEG, LANES), f32)
    ends = lax.fori_loop(0, seg, sweep1, tuple((zero, one) for _ in range(2 * nb)),
                         unroll=SWEEP_UNROLL)

    starts = []
    for k in range(2 * nb):
        h_end, p_end = ends[k]
        st = zero
        for _ in range(N_SEG - 1):
            st = _shift_rows(h_end + p_end * st, down=(k < nb))
        starts.append(st)

    def sweep2(j, carry):
        out = []
        for k in range(2 * nb):
            jj = j if k < nb else seg - 1 - j
            h = a_ref.at[k][row(jj)] * carry[k] + u_ref.at[k][row(jj)]
            h_ref.at[k][row(jj)] = h
            out.append(h)
        return tuple(out)

    lax.fori_loop(0, seg, sweep2, tuple(starts), unroll=SWEEP_UNROLL)

    def unscan(j, _):
        for n in range(nb):
            hsum = h_ref.at[n][row(j)] + h_ref.at[nb + n][row(j)]
            x_ref.at[n][pl.ds(SEG_OFF + j, N_SEG, stride=pitch), :] = hsum
        return 0

    lax.fori_loop(0, seg, unscan, 0, unroll=SWEEP_UNROLL)

    for s in range(N_SEG):
        hsum = jnp.concatenate(
            [x_ref[n, pl.ds(SEG_OFF + s * pitch, seg), :] for n in range(nb)], axis=1)
        o_ref[pl.ds(s * seg, seg), :] = _bf16(gy_ref[pl.ds(s * seg, seg), :].astype(f32) * hsum)


def _rglru(lx, gy, cw, cb, wg, bg, lam, batch, seq):
    t = lx.shape[0]
    pitch = seq // N_SEG + SEG_PAD
    row = lambda b: (b, 0)
    scan_buf = pltpu.VMEM((2 * LRU_BLOCKS, seq, LANES), jnp.float32)
    return pl.pallas_call(
        functools.partial(_rglru_kernel, seq=seq),
        grid=(batch,),
        in_specs=[
            pl.BlockSpec((seq, LRU_WIDTH), row), pl.BlockSpec((seq, LRU_WIDTH), row),
            _const_spec(cw.shape), _const_spec(cb.shape), _const_spec(wg.shape),
            _const_spec(bg.shape), _const_spec(lam.shape),
        ],
        out_specs=pl.BlockSpec((seq, LRU_WIDTH), row),
        out_shape=jax.ShapeDtypeStruct((t, LRU_WIDTH), jnp.bfloat16),
        scratch_shapes=[
            pltpu.VMEM((LRU_BLOCKS, SEG_OFF + N_SEG * pitch + SUBLANES, LANES), jnp.float32),
            scan_buf, scan_buf, scan_buf,
        ],
        compiler_params=pltpu.CompilerParams(
            dimension_semantics=("arbitrary",), vmem_limit_bytes=VMEM_LIMIT_BYTES),
        name="rglru",
    )(lx, gy, cw, cb, wg, bg, lam)


def _attn_kernel(qt_ref, k_ref, vt_ref, o_ref):
    def scores(hd):
        sl = slice(hd * HEAD_PAD, (hd + 1) * HEAD_PAD)
        return _dot(k_ref[:, sl], qt_ref[sl, :])

    def values(hd, pt):
        sl = slice(hd * HEAD_PAD, (hd + 1) * HEAD_PAD)
        ot = _dot(vt_ref[sl, :], pt)
        return ot[:V_HEAD_DIM] / ot[V_HEAD_DIM:V_HEAD_DIM + 1]

    outs = []
    st = scores(0)
    pt_prev = None
    for hd in range(N_HEADS):
        st_next = scores(hd + 1) if hd + 1 < N_HEADS else None
        if pt_prev is not None:
            outs.append(values(hd - 1, pt_prev))
        pt_prev = _bf16(jnp.exp2(st - jnp.max(st, axis=0, keepdims=True)))
        st = st_next
    outs.append(values(N_HEADS - 1, pt_prev))
    o_ref[...] = _bf16(jnp.concatenate(outs, axis=0).T)


def _attention(qt, k, vt, batch, seq):
    t = k.shape[0]
    tq = min(TQ_ATTN, seq)
    per_seq = seq // tq
    return pl.pallas_call(
        _attn_kernel,
        grid=(batch, per_seq),
        in_specs=[
            pl.BlockSpec((None, N_HEADS * HEAD_PAD, tq), lambda b, i: (b, 0, i)),
            pl.BlockSpec((seq, N_HEADS * HEAD_PAD), lambda b, i: (b, 0)),
            pl.BlockSpec((None, N_HEADS * HEAD_PAD, seq), lambda b, i: (b, 0, 0)),
        ],
        out_specs=pl.BlockSpec((tq, N_HEADS * V_HEAD_DIM), lambda b, i: (b * per_seq + i, 0)),
        out_shape=jax.ShapeDtypeStruct((t, N_HEADS * V_HEAD_DIM), jnp.bfloat16),
        compiler_params=pltpu.CompilerParams(
            dimension_semantics=("arbitrary", "arbitrary"), vmem_limit_bytes=VMEM_LIMIT_BYTES),
        name="attention",
    )(qt, k, vt)


def _merge_ffn_kernel(x_ref, attn_ref, rec_ref, gates_ref, woa_ref, wol_ref, wout_ref,
                      g2_ref, wg_ref, wu_ref, wd_ref, gf_ref, o_ref):
    d = x_ref.shape[1]
    a = _dot(attn_ref[...], woa_ref[...])
    r = _dot(rec_ref[...], wol_ref[...])
    gates = gates_ref[...].astype(jnp.float32)
    merged = gates[:, :d] * a + gates[:, d:] * r
    x1 = x_ref[...] + _dot(_bf16(merged), wout_ref[...])
    h2 = _bf16(_rms(x1, g2_ref[...]))
    act = jax.nn.silu(_dot(h2, wg_ref[...])) * _dot(h2, wu_ref[...])
    x2 = x1 + _dot(_bf16(act), wd_ref[...])
    o_ref[...] = _rms(x2, gf_ref[...])


def _merge_ffn(x2, attn, rec, gates, woa, wol, wout, g2, wg, wu, wd, gf):
    t, d = x2.shape
    tm = min(TM_OUT, t)
    row = lambda i: (i, 0)
    return pl.pallas_call(
        _merge_ffn_kernel,
        grid=(t // tm,),
        in_specs=[
            pl.BlockSpec((tm, d), row), pl.BlockSpec((tm, attn.shape[1]), row),
            pl.BlockSpec((tm, rec.shape[1]), row), pl.BlockSpec((tm, 2 * d), row),
            _const_spec(woa.shape), _const_spec(wol.shape), _const_spec(wout.shape),
            _const_spec(g2.shape), _const_spec(wg.shape), _const_spec(wu.shape),
            _const_spec(wd.shape), _const_spec(gf.shape),
        ],
        out_specs=pl.BlockSpec((tm, d), row),
        out_shape=jax.ShapeDtypeStruct((t, d), jnp.float32),
        compiler_params=pltpu.CompilerParams(
            dimension_semantics=("arbitrary",), vmem_limit_bytes=VMEM_LIMIT_BYTES),
        name="merge_ffn",
    )(x2, attn, rec, gates, woa, wol, wout, g2, wg, wu, wd, gf)


def _rope_slot_tables(seq):
    pos = jnp.arange(seq, dtype=jnp.float32)
    inv_freq = 1.0 / (ROPE_THETA ** (jnp.arange(0, QK_ROPE_DIM, 2, dtype=jnp.float32) / QK_ROPE_DIM))
    ang = pos[:, None] * inv_freq[None, :]
    cos, sin = jnp.cos(ang), jnp.sin(ang)
    half = QK_ROPE_DIM // 2
    ones = jnp.ones((seq, QK_NOPE_DIM), jnp.float32)
    z_half = jnp.zeros((seq, half), jnp.float32)
    z_nope = jnp.zeros((seq, QK_NOPE_DIM), jnp.float32)
    z_pad = jnp.zeros((seq, HEAD_PAD - QK_DIM), jnp.float32)
    rc = jnp.concatenate([ones, cos, cos, z_pad], axis=1)
    rlo = jnp.concatenate([z_nope, -sin, z_half, z_pad], axis=1)
    rhi = jnp.concatenate([z_nope, z_half, sin, z_pad], axis=1)
    return rc, rlo, rhi


def _head_slots(w, width):
    k = w.shape[0]
    w = w.reshape(k, N_HEADS, width)
    w = jnp.pad(w, ((0, 0), (0, 0), (0, HEAD_PAD - width)))
    return w.reshape(k, N_HEADS * HEAD_PAD)


def kernel(x, norm1_g, w_in, q_a_norm_g, w_q_b, kv_a_norm_g, w_kv_b, w_o_attn,
           conv_w, conv_b, w_rgate, b_rgate, w_igate, b_igate, lru_lambda, w_o_lru,
           w_out, norm2_g, w_ffn_gate, w_ffn_up, w_ffn_down, final_g):
    batch, seq, d = x.shape
    depth = w_in.shape[0]
    bf = jnp.bfloat16
    c_qa = Q_LORA_RANK
    c_kva = c_qa + KV_LORA_RANK
    c_kr = c_kva + QK_ROPE_DIM
    rc, rlo, rhi = _rope_slot_tables(seq)
    assert depth == 1, "the final rmsnorm is fused into the single layer's last kernel"
    l = 0
    x2 = x.reshape(batch * seq, d)

    wi = w_in[l]
    w_kr = jnp.pad(wi[:, c_kva:c_kr], ((0, 0), (QK_NOPE_DIM, HEAD_PAD - QK_DIM)))
    w_in_r = jnp.concatenate([wi[:, :c_kva], w_kr, wi[:, c_kr:]], axis=1).astype(bf)
    wqb_r = _head_slots(w_q_b[l], QK_DIM).astype(bf)
    wkv = w_kv_b[l].reshape(KV_LORA_RANK, N_HEADS, QK_NOPE_DIM + V_HEAD_DIM)
    wk_r = _head_slots(wkv[:, :, :QK_NOPE_DIM].reshape(KV_LORA_RANK, -1), QK_NOPE_DIM)
    wv_r = _head_slots(wkv[:, :, QK_NOPE_DIM:].reshape(KV_LORA_RANK, -1), V_HEAD_DIM)
    wkvb_r = jnp.concatenate([wk_r, wv_r], axis=1).astype(bf)

    qt, k, vt, lx, gy, gates = _in_proj(
        x2, norm1_g[l][None], w_in_r, q_a_norm_g[l][None], wqb_r, kv_a_norm_g[l][None],
        wkvb_r, rc, rlo, rhi, batch, seq)

    wg = (0.5 * jnp.concatenate([w_rgate[l, 0], w_rgate[l, 1], w_igate[l, 0], w_igate[l, 1]], axis=2)).astype(bf)
    bg = 0.5 * jnp.concatenate([b_rgate[l, 0], b_rgate[l, 1], b_igate[l, 0], b_igate[l, 1]], axis=1)[:, None, :]
    rec = _rglru(lx, gy, conv_w[l], conv_b[l][None], wg, bg, lru_lambda[l], batch, seq)

    attn = _attention(qt, k, vt, batch, seq)

    out = _merge_ffn(x2, attn, rec, gates, w_o_attn[l].astype(bf), w_o_lru[l].astype(bf),
                     w_out[l].astype(bf), norm2_g[l][None], w_ffn_gate[l].astype(bf),
                     w_ffn_up[l].astype(bf), w_ffn_down[l].astype(bf), final_g[None])
    return out.reshape(batch, seq, d)
```

```python
import functools

import jax
import jax.numpy as jnp
from jax import lax
from jax.experimental import pallas as pl
from jax.experimental.pallas import tpu as pltpu

N_HEADS = 8
QK_NOPE_DIM = 64
QK_ROPE_DIM = 32
QK_DIM = QK_NOPE_DIM + QK_ROPE_DIM
V_HEAD_DIM = 64
Q_LORA_RANK = 256
KV_LORA_RANK = 128
ROPE_THETA = 10000.0
LRU_WIDTH = 512
LRU_BLOCKS = 4
LRU_BLOCK_DIM = 128
CONV_WIDTH = 4
CONV_LEFT = 2
LRU_C = 8.0
NORM_EPS = 1e-6
LOG2_E = 1.4426950408889634

LANES = 128
SUBLANES = 8
HEAD_PAD = LANES
VMEM_LIMIT_BYTES = 56 * 1024 * 1024

TM_IN = 512
TQ_ATTN = 512
TK_ATTN = 256
TM_OUT = 512
N_SEG = SUBLANES
SEG_PAD = 4
SWEEP_UNROLL = 4
SEG_OFF = SUBLANES
SCAN_CHUNK = 32


def _sigmoid(x):
    return 0.5 * jnp.tanh(0.5 * x) + 0.5


def _rms(x, g):
    return x * lax.rsqrt(jnp.mean(x * x, axis=-1, keepdims=True) + NORM_EPS) * g


def _bf16(x):
    return x.astype(jnp.bfloat16)


def _dot(a, b):
    return jnp.dot(a, b, preferred_element_type=jnp.float32)


def _const_spec(shape):
    nd = len(shape)
    return pl.BlockSpec(shape, lambda *_: (0,) * nd, pipeline_mode=pl.Buffered(1))


def _rope_slot(t, c, s_lo, s_hi):
    return t * c + pltpu.roll(t, 16, 1) * s_hi + pltpu.roll(t, LANES - 16, 1) * s_lo


def _in_proj_kernel(x_ref, g1_ref, w_in_ref, gq_ref, wqb_ref, gkv_ref, wkvb_ref,
                    rc_ref, rlo_ref, rhi_ref,
                    qt_ref, k_ref, vt_ref, lx_ref, gy_ref, gates_ref, lat_ref):
    c0 = Q_LORA_RANK
    c1 = c0 + KV_LORA_RANK
    c2 = c1 + HEAD_PAD
    c3 = c2 + LRU_WIDTH
    c4 = c3 + LRU_WIDTH

    @pl.when(pl.program_id(0) == 0)
    def _():
        lat_ref[...] = jnp.zeros(lat_ref.shape, lat_ref.dtype)

    q_a = lat_ref[:, :c0]
    kv_a = lat_ref[:, c0:c1]
    k_rope = lat_ref[:, c1:c2]
    rc = rc_ref[...]
    rlo = rlo_ref[...]
    rhi = rhi_ref[...]
    scale = QK_DIM ** -0.5 * LOG2_E
    q = _dot(_bf16(_rms(q_a, gq_ref[...])), wqb_ref[...])
    kv = _dot(_bf16(_rms(kv_a, gkv_ref[...])), wkvb_ref[...])
    k_rope = _rope_slot(k_rope, rc, rlo, rhi)
    ones_lane = (lax.broadcasted_iota(jnp.int32, (1, HEAD_PAD), 1) == V_HEAD_DIM).astype(jnp.float32)
    for hd in range(N_HEADS):
        sl = slice(hd * HEAD_PAD, (hd + 1) * HEAD_PAD)
        vsl = slice((N_HEADS + hd) * HEAD_PAD, (N_HEADS + hd + 1) * HEAD_PAD)
        qt_ref[sl, :] = _bf16((_rope_slot(q[:, sl], rc, rlo, rhi) * scale).T)
        k_ref[:, sl] = _bf16(kv[:, sl] + k_rope)
        vt_ref[sl, :] = _bf16((kv[:, vsl] + ones_lane).T)

    half = x_ref.shape[0] // 2
    for r in range(2):
        rows = pl.ds(r * half, half)
        h = _bf16(_rms(x_ref[rows, :], g1_ref[...]))
        proj = _dot(h, w_in_ref[...])
        lat_ref[rows, :] = proj[:, :c2]
        lx_ref[rows, :] = proj[:, c2:c3]
        gy_ref[rows, :] = _bf16(jax.nn.gelu(proj[:, c3:c4]))
        gates_ref[rows, :] = _bf16(jax.nn.sigmoid(proj[:, c4:]))


def _in_proj(x2, g1, w_in_r, gq, wqb_r, gkv, wkvb_r, rc, rlo, rhi, batch, seq):
    t = x2.shape[0]
    d = x2.shape[1]
    tm = min(TM_IN, seq)
    per_seq = seq // tm
    n = t // tm
    cur = lambda i: jnp.minimum(i, n - 1)
    prev = lambda i: jnp.maximum(i - 1, 0)
    row = lambda i: (cur(i), 0)
    prow = lambda i: (prev(i), 0)
    ppos = lambda i: (prev(i) % per_seq, 0)
    pcol = lambda i: (prev(i) // per_seq, 0, prev(i) % per_seq)
    bf = jnp.bfloat16
    return pl.pallas_call(
        _in_proj_kernel,
        grid=(n + 1,),
        in_specs=[
            pl.BlockSpec((tm, d), row),
            _const_spec(g1.shape), _const_spec(w_in_r.shape), _const_spec(gq.shape),
            _const_spec(wqb_r.shape), _const_spec(gkv.shape), _const_spec(wkvb_r.shape),
            pl.BlockSpec((tm, LANES), ppos), pl.BlockSpec((tm, LANES), ppos),
            pl.BlockSpec((tm, LANES), ppos),
        ],
        out_specs=[
            pl.BlockSpec((None, N_HEADS * HEAD_PAD, tm), pcol),
            pl.BlockSpec((tm, N_HEADS * HEAD_PAD), prow),
            pl.BlockSpec((None, N_HEADS * HEAD_PAD, tm), pcol),
            pl.BlockSpec((tm, LRU_WIDTH), row),
            pl.BlockSpec((tm, LRU_WIDTH), row),
            pl.BlockSpec((tm, 2 * d), row),
        ],
        out_shape=[
            jax.ShapeDtypeStruct((batch, N_HEADS * HEAD_PAD, seq), bf),
            jax.ShapeDtypeStruct((t, N_HEADS * HEAD_PAD), bf),
            jax.ShapeDtypeStruct((batch, N_HEADS * HEAD_PAD, seq), bf),
            jax.ShapeDtypeStruct((t, LRU_WIDTH), jnp.float32),
            jax.ShapeDtypeStruct((t, LRU_WIDTH), bf),
            jax.ShapeDtypeStruct((t, 2 * d), bf),
        ],
        scratch_shapes=[pltpu.VMEM((tm, Q_LORA_RANK + KV_LORA_RANK + HEAD_PAD), jnp.float32)],
        compiler_params=pltpu.CompilerParams(
            dimension_semantics=("arbitrary",), vmem_limit_bytes=VMEM_LIMIT_BYTES),
        name="in_proj",
    )(x2, g1, w_in_r, gq, wqb_r, gkv, wkvb_r, rc, rlo, rhi)


def _shift_rows(x, down):
    rows = lax.broadcasted_iota(jnp.int32, x.shape, 0)
    if down:
        return jnp.where(rows == 0, 0.0, pltpu.roll(x, 1, 0))
    return jnp.where(rows == x.shape[0] - 1, 0.0, pltpu.roll(x, x.shape[0] - 1, 0))


def _rglru_kernel(lx_ref, gy_ref, cw_ref, cb_ref, wg_ref, bg_ref, lam_ref, o_ref,
                  x_ref, a_ref, u_ref, h_ref, *, seq):
    seg = seq // N_SEG
    pitch = seg + SEG_PAD
    nb = LRU_BLOCKS
    jc = min(SCAN_CHUNK, seg)
    f32 = jnp.float32

    zeros = jnp.zeros((SUBLANES, LANES), f32)
    for n in range(nb):
        ln = slice(n * LANES, (n + 1) * LANES)
        x_ref[n, pl.ds(0, SUBLANES), :] = zeros
        x_ref[n, pl.ds(SEG_OFF + N_SEG * pitch - SEG_PAD, SUBLANES), :] = zeros
        for s in range(N_SEG):
            x_ref[n, pl.ds(SEG_OFF + s * pitch, seg), :] = lx_ref[pl.ds(s * seg, seg), ln]
        for s in range(1, N_SEG):
            x_ref[n, pl.ds(SEG_OFF + s * pitch - CONV_LEFT, CONV_LEFT), :] = (
                lx_ref[pl.ds(s * seg - CONV_LEFT, CONV_LEFT), ln])
            x_ref[n, pl.ds(SEG_OFF + (s - 1) * pitch + seg, 1), :] = lx_ref[pl.ds(s * seg, 1), ln]

    lam = lam_ref[...]
    z = -lam
    softplus = jnp.maximum(z, 0.0) + jnp.log1p(jnp.exp(-jnp.abs(z)))
    half_decay = (-0.5 * LRU_C) * softplus
    cw = cw_ref[...]
    cb = cb_ref[...]

    def gates(c, _):
        j0 = c * jc
        for n in range(nb):
            ln = slice(n * LANES, (n + 1) * LANES)
            xv = [x_ref.at[n][pl.ds(SEG_OFF - CONV_LEFT + j0 + r, N_SEG, stride=pitch), :]
                  for r in range(jc + CONV_WIDTH - 1)]
            xc = cb[:, ln] + sum(
                jnp.concatenate(xv[o:o + jc], axis=0) * cw[o:o + 1, ln] for o in range(CONV_WIDTH))
            t = jnp.tanh(_dot(_bf16(xc), wg_ref[n]) + bg_ref[n])
            hx = 0.5 * xc
            rows = pl.ds(pl.multiple_of(j0 * N_SEG, jc * N_SEG), jc * N_SEG)
            for d in range(2):
                hd = half_decay[d:d + 1, ln]
                log_a = t[:, d * LANES:(d + 1) * LANES] * hd + hd
                a = jnp.exp(log_a)
                m2 = 1.0 - a * a
                mult = jnp.where(m2 > 0.0, m2 * lax.rsqrt(m2), 0.0)
                a_ref[d * nb + n, rows, :] = a
                u_ref[d * nb + n, rows, :] = (t[:, (2 + d) * LANES:(3 + d) * LANES] + 1.0) * (hx * mult)
        return 0

    lax.fori_loop(0, seg // jc, gates, 0)

    def row(j):
        return (pl.ds(pl.multiple_of(j * N_SEG, N_SEG), N_SEG), slice(None))

    def sweep1(j, carry):
        out = []
        for k in range(2 * nb):
            h, p = carry[k]
            jj = j if k < nb else seg - 1 - j
            a = a_ref.at[k][row(jj)]
            out.append((a * h + u_ref.at[k][row(jj)], a * p))
        return tuple(out)

    zero = jnp.zeros((N_SEG, LANES), f32)
    one = jnp.ones((N_SEG, LANES), f32)
    ends = lax.fori_loop(0, seg, sweep1, tuple((zero, one) for _ in range(2 * nb)),
                         unroll=SWEEP_UNROLL)

    starts = []
    for k in range(2 * nb):
        h_end, p_end = ends[k]
        st = zero
        for _ in range(N_SEG - 1):
            st = _shift_rows(h_end + p_end * st, down=(k < nb))
        starts.append(st)

    def sweep2(j, carry):
        out = []
        for k in range(2 * nb):
            jj = j if k < nb else seg - 1 - j
            h = a_ref.at[k][row(jj)] * carry[k] + u_ref.at[k][row(jj)]
            h_ref.at[k][row(jj)] = h
            out.append(h)
        return tuple(out)

    lax.fori_loop(0, seg, sweep2, tuple(starts), unroll=SWEEP_UNROLL)

    def unscan(j, _):
        for n in range(nb):
            hsum = h_ref.at[n][row(j)] + h_ref.at[nb + n][row(j)]
            x_ref.at[n][pl.ds(SEG_OFF + j, N_SEG, stride=pitch), :] = hsum
        return 0

    lax.fori_loop(0, seg, unscan, 0, unroll=SWEEP_UNROLL)

    for s in range(N_SEG):
        hsum = jnp.concatenate(
            [x_ref[n, pl.ds(SEG_OFF + s * pitch, seg), :] for n in range(nb)], axis=1)
        o_ref[pl.ds(s * seg, seg), :] = _bf16(gy_ref[pl.ds(s * seg, seg), :].astype(f32) * hsum)


def _rglru(lx, gy, cw, cb, wg, bg, lam, batch, seq):
    t = lx.shape[0]
    pitch = seq // N_SEG + SEG_PAD
    row = lambda b: (b, 0)
    scan_buf = pltpu.VMEM((2 * LRU_BLOCKS, seq, LANES), jnp.float32)
    return pl.pallas_call(
        functools.partial(_rglru_kernel, seq=seq),
        grid=(batch,),
        in_specs=[
            pl.BlockSpec((seq, LRU_WIDTH), row), pl.BlockSpec((seq, LRU_WIDTH), row),
            _const_spec(cw.shape), _const_spec(cb.shape), _const_spec(wg.shape),
            _const_spec(bg.shape), _const_spec(lam.shape),
        ],
        out_specs=pl.BlockSpec((seq, LRU_WIDTH), row),
        out_shape=jax.ShapeDtypeStruct((t, LRU_WIDTH), jnp.bfloat16),
        scratch_shapes=[
            pltpu.VMEM((LRU_BLOCKS, SEG_OFF + N_SEG * pitch + SUBLANES, LANES), jnp.float32),
            scan_buf, scan_buf, scan_buf,
        ],
        compiler_params=pltpu.CompilerParams(
            dimension_semantics=("arbitrary",), vmem_limit_bytes=VMEM_LIMIT_BYTES),
        name="rglru",
    )(lx, gy, cw, cb, wg, bg, lam)


def _attn_kernel(qt_ref, k_ref, vt_ref, o_ref):
    seq = k_ref.shape[0]
    nblk = seq // TK_ATTN
    heads = [slice(hd * HEAD_PAD, (hd + 1) * HEAD_PAD) for hd in range(N_HEADS)]
    blocks = [slice(j * TK_ATTN, (j + 1) * TK_ATTN) for j in range(nblk)]
    st = {}
    pt = {}
    mx = {}
    acc = {}
    outs = []
    for it in range(N_HEADS + 2):
        ha, hb, hc = it, it - 1, it - 2
        for j in range(nblk):
            if ha < N_HEADS:
                st[ha, j] = _dot(k_ref[blocks[j], heads[ha]], qt_ref[heads[ha], :])
                bm = st[ha, j].max(axis=0, keepdims=True)
                mx[ha] = bm if j == 0 else jnp.maximum(mx[ha], bm)
            if 0 <= hb < N_HEADS:
                pt[hb, j] = _bf16(jnp.exp2(st.pop((hb, j)) - mx[hb]))
            if 0 <= hc < N_HEADS:
                part = _dot(vt_ref[heads[hc], blocks[j]], pt.pop((hc, j)))
                acc[hc] = part if j == 0 else acc[hc] + part
        if 0 <= hc < N_HEADS:
            a = acc.pop(hc)
            outs.append(a[:V_HEAD_DIM] / a[V_HEAD_DIM:V_HEAD_DIM + 1])
    o_ref[...] = _bf16(jnp.concatenate(outs, axis=0).T)


def _attention(qt, k, vt, batch, seq):
    t = k.shape[0]
    tq = min(TQ_ATTN, seq)
    per_seq = seq // tq
    return pl.pallas_call(
        _attn_kernel,
        grid=(batch, per_seq),
        in_specs=[
            pl.BlockSpec((None, N_HEADS * HEAD_PAD, tq), lambda b, i: (b, 0, i)),
            pl.BlockSpec((seq, N_HEADS * HEAD_PAD), lambda b, i: (b, 0)),
            pl.BlockSpec((None, N_HEADS * HEAD_PAD, seq), lambda b, i: (b, 0, 0)),
        ],
        out_specs=pl.BlockSpec((tq, N_HEADS * V_HEAD_DIM), lambda b, i: (b * per_seq + i, 0)),
        out_shape=jax.ShapeDtypeStruct((t, N_HEADS * V_HEAD_DIM), jnp.bfloat16),
        compiler_params=pltpu.CompilerParams(
            dimension_semantics=("arbitrary", "arbitrary"), vmem_limit_bytes=VMEM_LIMIT_BYTES),
        name="attention",
    )(qt, k, vt)


def _merge_ffn_kernel(x_ref, attn_ref, rec_ref, gates_ref, woa_ref, wol_ref, wout_ref,
                      g2_ref, wg_ref, wu_ref, wd_ref, gf_ref, o_ref):
    d = x_ref.shape[1]
    half = x_ref.shape[0] // 2
    rows = [pl.ds(r * half, half) for r in range(2)]
    a = [_dot(attn_ref[rw, :], woa_ref[...]) for rw in rows]
    r = [_dot(rec_ref[rw, :], wol_ref[...]) for rw in rows]
    merged = []
    for i, rw in enumerate(rows):
        gates = gates_ref[rw, :].astype(jnp.float32)
        merged.append(_bf16(gates[:, :d] * a[i] + gates[:, d:] * r[i]))
    x1 = [x_ref[rw, :] + _dot(merged[i], wout_ref[...]) for i, rw in enumerate(rows)]
    h2 = [_bf16(_rms(v, g2_ref[...])) for v in x1]
    gate = [_dot(v, wg_ref[...]) for v in h2]
    up = [_dot(v, wu_ref[...]) for v in h2]
    act = [_bf16(jax.nn.silu(gate[i]) * up[i]) for i in range(2)]
    for i, rw in enumerate(rows):
        o_ref[rw, :] = _rms(x1[i] + _dot(act[i], wd_ref[...]), gf_ref[...])


def _merge_ffn(x2, attn, rec, gates, woa, wol, wout, g2, wg, wu, wd, gf):
    t, d = x2.shape
    tm = min(TM_OUT, t)
    row = lambda i: (i, 0)
    return pl.pallas_call(
        _merge_ffn_kernel,
        grid=(t // tm,),
        in_specs=[
            pl.BlockSpec((tm, d), row), pl.BlockSpec((tm, attn.shape[1]), row),
            pl.BlockSpec((tm, rec.shape[1]), row), pl.BlockSpec((tm, 2 * d), row),
            _const_spec(woa.shape), _const_spec(wol.shape), _const_spec(wout.shape),
            _const_spec(g2.shape), _const_spec(wg.shape), _const_spec(wu.shape),
            _const_spec(wd.shape), _const_spec(gf.shape),
        ],
        out_specs=pl.BlockSpec((tm, d), row),
        out_shape=jax.ShapeDtypeStruct((t, d), jnp.float32),
        compiler_params=pltpu.CompilerParams(
            dimension_semantics=("arbitrary",), vmem_limit_bytes=VMEM_LIMIT_BYTES),
        name="merge_ffn",
    )(x2, attn, rec, gates, woa, wol, wout, g2, wg, wu, wd, gf)


def _rope_slot_tables(seq):
    pos = jnp.arange(seq, dtype=jnp.float32)
    inv_freq = 1.0 / (ROPE_THETA ** (jnp.arange(0, QK_ROPE_DIM, 2, dtype=jnp.float32) / QK_ROPE_DIM))
    ang = pos[:, None] * inv_freq[None, :]
    cos, sin = jnp.cos(ang), jnp.sin(ang)
    half = QK_ROPE_DIM // 2
    ones = jnp.ones((seq, QK_NOPE_DIM), jnp.float32)
    z_half = jnp.zeros((seq, half), jnp.float32)
    z_nope = jnp.zeros((seq, QK_NOPE_DIM), jnp.float32)
    z_pad = jnp.zeros((seq, HEAD_PAD - QK_DIM), jnp.float32)
    rc = jnp.concatenate([ones, cos, cos, z_pad], axis=1)
    rlo = jnp.concatenate([z_nope, -sin, z_half, z_pad], axis=1)
    rhi = jnp.concatenate([z_nope, z_half, sin, z_pad], axis=1)
    return rc, rlo, rhi


def _head_slots(w, width):
    k = w.shape[0]
    w = w.reshape(k, N_HEADS, width)
    w = jnp.pad(w, ((0, 0), (0, 0), (0, HEAD_PAD - width)))
    return w.reshape(k, N_HEADS * HEAD_PAD)


def kernel(x, norm1_g, w_in, q_a_norm_g, w_q_b, kv_a_norm_g, w_kv_b, w_o_attn,
           conv_w, conv_b, w_rgate, b_rgate, w_igate, b_igate, lru_lambda, w_o_lru,
           w_out, norm2_g, w_ffn_gate, w_ffn_up, w_ffn_down, final_g):
    batch, seq, d = x.shape
    depth = w_in.shape[0]
    bf = jnp.bfloat16
    c_qa = Q_LORA_RANK
    c_kva = c_qa + KV_LORA_RANK
    c_kr = c_kva + QK_ROPE_DIM
    rc, rlo, rhi = _rope_slot_tables(seq)
    assert depth == 1, "the final rmsnorm is fused into the single layer's last kernel"
    l = 0
    x2 = x.reshape(batch * seq, d)

    wi = w_in[l]
    w_kr = jnp.pad(wi[:, c_kva:c_kr], ((0, 0), (QK_NOPE_DIM, HEAD_PAD - QK_DIM)))
    w_in_r = jnp.concatenate([wi[:, :c_kva], w_kr, wi[:, c_kr:]], axis=1).astype(bf)
    wqb_r = _head_slots(w_q_b[l], QK_DIM).astype(bf)
    wkv = w_kv_b[l].reshape(KV_LORA_RANK, N_HEADS, QK_NOPE_DIM + V_HEAD_DIM)
    wk_r = _head_slots(wkv[:, :, :QK_NOPE_DIM].reshape(KV_LORA_RANK, -1), QK_NOPE_DIM)
    wv_r = _head_slots(wkv[:, :, QK_NOPE_DIM:].reshape(KV_LORA_RANK, -1), V_HEAD_DIM)
    wkvb_r = jnp.concatenate([wk_r, wv_r], axis=1).astype(bf)

    qt, k, vt, lx, gy, gates = _in_proj(
        x2, norm1_g[l][None], w_in_r, q_a_norm_g[l][None], wqb_r, kv_a_norm_g[l][None],
        wkvb_r, rc, rlo, rhi, batch, seq)

    wg = (0.5 * jnp.concatenate([w_rgate[l, 0], w_rgate[l, 1], w_igate[l, 0], w_igate[l, 1]], axis=2)).astype(bf)
    bg = 0.5 * jnp.concatenate([b_rgate[l, 0], b_rgate[l, 1], b_igate[l, 0], b_igate[l, 1]], axis=1)[:, None, :]
    rec = _rglru(lx, gy, conv_w[l], conv_b[l][None], wg, bg, lru_lambda[l], batch, seq)

    attn = _attention(qt, k, vt, batch, seq)

    out = _merge_ffn(x2, attn, rec, gates, w_o_attn[l].astype(bf), w_o_lru[l].astype(bf),
                     w_out[l].astype(bf), norm2_g[l][None], w_ffn_gate[l].astype(bf),
                     w_ffn_up[l].astype(bf), w_ffn_down[l].astype(bf), final_g[None])
    return out.reshape(batch, seq, d)
```

```python
import functools

import jax
import jax.numpy as jnp
from jax import lax
from jax.experimental import pallas as pl
from jax.experimental.pallas import tpu as pltpu

N_HEADS = 8
QK_NOPE_DIM = 64
QK_ROPE_DIM = 32
QK_DIM = QK_NOPE_DIM + QK_ROPE_DIM
V_HEAD_DIM = 64
Q_LORA_RANK = 256
KV_LORA_RANK = 128
ROPE_THETA = 10000.0
LRU_WIDTH = 512
LRU_BLOCKS = 4
LRU_BLOCK_DIM = 128
CONV_WIDTH = 4
CONV_LEFT = 2
LRU_C = 8.0
NORM_EPS = 1e-6
LOG2_E = 1.4426950408889634

LANES = 128
SUBLANES = 8
HEAD_PAD = LANES
V_SLOT = 80
VMEM_LIMIT_BYTES = 56 * 1024 * 1024

TM_IN = 512
TQ_ATTN = 512
TK_ATTN = 256
TM_OUT = 512
N_SEG = SUBLANES
SEG_PAD = 4
SWEEP_UNROLL = 4
SEG_OFF = SUBLANES
SCAN_CHUNK = 32


def _sigmoid(x):
    return 0.5 * jnp.tanh(0.5 * x) + 0.5


def _rms(x, g):
    return x * lax.rsqrt(jnp.mean(x * x, axis=-1, keepdims=True) + NORM_EPS) * g


def _bf16(x):
    return x.astype(jnp.bfloat16)


def _dot(a, b):
    return jnp.dot(a, b, preferred_element_type=jnp.float32)


def _const_spec(shape):
    nd = len(shape)
    return pl.BlockSpec(shape, lambda *_: (0,) * nd, pipeline_mode=pl.Buffered(1))


def _rope_slot(t, c, s_lo, s_hi):
    return t * c + pltpu.roll(t, 16, 1) * s_hi + pltpu.roll(t, LANES - 16, 1) * s_lo


def _in_proj_kernel(x_ref, g1_ref, w_in_ref, gq_ref, wqb_ref, gkv_ref, wkvb_ref,
                    rc_ref, rlo_ref, rhi_ref,
                    qt_ref, k_ref, vt_ref, lx_ref, gy_ref, gates_ref, lat_ref):
    c0 = Q_LORA_RANK
    c1 = c0 + KV_LORA_RANK
    c2 = c1 + HEAD_PAD
    c3 = c2 + LRU_WIDTH
    c4 = c3 + LRU_WIDTH

    @pl.when(pl.program_id(0) == 0)
    def _():
        lat_ref[...] = jnp.zeros(lat_ref.shape, lat_ref.dtype)

    q_a = lat_ref[:, :c0]
    kv_a = lat_ref[:, c0:c1]
    k_rope = lat_ref[:, c1:c2]
    rc = rc_ref[...]
    rlo = rlo_ref[...]
    rhi = rhi_ref[...]
    scale = QK_DIM ** -0.5 * LOG2_E
    q = _dot(_bf16(_rms(q_a, gq_ref[...])), wqb_ref[...])
    kv = _dot(_bf16(_rms(kv_a, gkv_ref[...])), wkvb_ref[...])
    k_rope = _rope_slot(k_rope, rc, rlo, rhi)
    vt = _bf16(kv[:, N_HEADS * HEAD_PAD:].T)
    pad_rows = V_SLOT - V_HEAD_DIM
    ones_row = (lax.broadcasted_iota(jnp.int32, (pad_rows, vt.shape[1]), 0) == 0).astype(vt.dtype)
    for hd in range(N_HEADS):
        sl = slice(hd * HEAD_PAD, (hd + 1) * HEAD_PAD)
        qt_ref[sl, :] = _bf16((_rope_slot(q[:, sl], rc, rlo, rhi) * scale).T)
        k_ref[:, sl] = _bf16(kv[:, sl] + k_rope)
        vt_ref[pl.ds(hd * V_SLOT, V_HEAD_DIM), :] = vt[hd * V_HEAD_DIM:(hd + 1) * V_HEAD_DIM]
        vt_ref[pl.ds(hd * V_SLOT + V_HEAD_DIM, pad_rows), :] = ones_row

    half = x_ref.shape[0] // 2
    for r in range(2):
        rows = pl.ds(r * half, half)
        h = _bf16(_rms(x_ref[rows, :], g1_ref[...]))
        proj = _dot(h, w_in_ref[...])
        lat_ref[rows, :] = proj[:, :c2]
        lx_ref[rows, :] = proj[:, c2:c3]
        gy_ref[rows, :] = _bf16(jax.nn.gelu(proj[:, c3:c4]))
        gates_ref[rows, :] = _bf16(jax.nn.sigmoid(proj[:, c4:]))


def _in_proj(x2, g1, w_in_r, gq, wqb_r, gkv, wkvb_r, rc, rlo, rhi, batch, seq):
    t = x2.shape[0]
    d = x2.shape[1]
    tm = min(TM_IN, seq)
    per_seq = seq // tm
    n = t // tm
    cur = lambda i: jnp.minimum(i, n - 1)
    prev = lambda i: jnp.maximum(i - 1, 0)
    row = lambda i: (cur(i), 0)
    prow = lambda i: (prev(i), 0)
    ppos = lambda i: (prev(i) % per_seq, 0)
    pcol = lambda i: (prev(i) // per_seq, 0, prev(i) % per_seq)
    bf = jnp.bfloat16
    return pl.pallas_call(
        _in_proj_kernel,
        grid=(n + 1,),
        in_specs=[
            pl.BlockSpec((tm, d), row),
            _const_spec(g1.shape), _const_spec(w_in_r.shape), _const_spec(gq.shape),
            _const_spec(wqb_r.shape), _const_spec(gkv.shape), _const_spec(wkvb_r.shape),
            pl.BlockSpec((tm, LANES), ppos), pl.BlockSpec((tm, LANES), ppos),
            pl.BlockSpec((tm, LANES), ppos),
        ],
        out_specs=[
            pl.BlockSpec((None, N_HEADS * HEAD_PAD, tm), pcol),
            pl.BlockSpec((tm, N_HEADS * HEAD_PAD), prow),
            pl.BlockSpec((None, N_HEADS * V_SLOT, tm), pcol),
            pl.BlockSpec((tm, LRU_WIDTH), row),
            pl.BlockSpec((tm, LRU_WIDTH), row),
            pl.BlockSpec((tm, 2 * d), row),
        ],
        out_shape=[
            jax.ShapeDtypeStruct((batch, N_HEADS * HEAD_PAD, seq), bf),
            jax.ShapeDtypeStruct((t, N_HEADS * HEAD_PAD), bf),
            jax.ShapeDtypeStruct((batch, N_HEADS * V_SLOT, seq), bf),
            jax.ShapeDtypeStruct((t, LRU_WIDTH), jnp.float32),
            jax.ShapeDtypeStruct((t, LRU_WIDTH), bf),
            jax.ShapeDtypeStruct((t, 2 * d), bf),
        ],
        scratch_shapes=[pltpu.VMEM((tm, Q_LORA_RANK + KV_LORA_RANK + HEAD_PAD), jnp.float32)],
        compiler_params=pltpu.CompilerParams(
            dimension_semantics=("arbitrary",), vmem_limit_bytes=VMEM_LIMIT_BYTES),
        name="in_proj",
    )(x2, g1, w_in_r, gq, wqb_r, gkv, wkvb_r, rc, rlo, rhi)


def _shift_rows(x, down):
    rows = lax.broadcasted_iota(jnp.int32, x.shape, 0)
    if down:
        return jnp.where(rows == 0, 0.0, pltpu.roll(x, 1, 0))
    return jnp.where(rows == x.shape[0] - 1, 0.0, pltpu.roll(x, x.shape[0] - 1, 0))


def _rglru_kernel(lx_ref, gy_ref, cw_ref, cb_ref, wg_ref, bg_ref, lam_ref, o_ref,
                  x_ref, a_ref, u_ref, h_ref, *, seq):
    seg = seq // N_SEG
    pitch = seg + SEG_PAD
    nb = LRU_BLOCKS
    jc = min(SCAN_CHUNK, seg)
    f32 = jnp.float32

    zeros = jnp.zeros((SUBLANES, LANES), f32)
    for n in range(nb):
        ln = slice(n * LANES, (n + 1) * LANES)
        x_ref[n, pl.ds(0, SUBLANES), :] = zeros
        x_ref[n, pl.ds(SEG_OFF + N_SEG * pitch - SEG_PAD, SUBLANES), :] = zeros
        for s in range(N_SEG):
            x_ref[n, pl.ds(SEG_OFF + s * pitch, seg), :] = lx_ref[pl.ds(s * seg, seg), ln]
        for s in range(1, N_SEG):
            x_ref[n, pl.ds(SEG_OFF + s * pitch - CONV_LEFT, CONV_LEFT), :] = (
                lx_ref[pl.ds(s * seg - CONV_LEFT, CONV_LEFT), ln])
            x_ref[n, pl.ds(SEG_OFF + (s - 1) * pitch + seg, 1), :] = lx_ref[pl.ds(s * seg, 1), ln]

    lam = lam_ref[...]
    z = -lam
    softplus = jnp.maximum(z, 0.0) + jnp.log1p(jnp.exp(-jnp.abs(z)))
    half_decay = (-0.5 * LRU_C) * softplus
    cw = cw_ref[...]
    cb = cb_ref[...]

    def gates(c, _):
        j0 = c * jc
        for n in range(nb):
            ln = slice(n * LANES, (n + 1) * LANES)
            xv = [x_ref.at[n][pl.ds(SEG_OFF - CONV_LEFT + j0 + r, N_SEG, stride=pitch), :]
                  for r in range(jc + CONV_WIDTH - 1)]
            xc = cb[:, ln] + sum(
                jnp.concatenate(xv[o:o + jc], axis=0) * cw[o:o + 1, ln] for o in range(CONV_WIDTH))
            t = jnp.tanh(_dot(_bf16(xc), wg_ref[n]) + bg_ref[n])
            hx = 0.5 * xc
            rows = pl.ds(pl.multiple_of(j0 * N_SEG, jc * N_SEG), jc * N_SEG)
            for d in range(2):
                hd = half_decay[d:d + 1, ln]
                log_a = t[:, d * LANES:(d + 1) * LANES] * hd + hd
                a = jnp.exp(log_a)
                m2 = 1.0 - a * a
                mult = jnp.where(m2 > 0.0, m2 * lax.rsqrt(m2), 0.0)
                a_ref[d * nb + n, rows, :] = a
                u_ref[d * nb + n, rows, :] = (t[:, (2 + d) * LANES:(3 + d) * LANES] + 1.0) * (hx * mult)
        return 0

    lax.fori_loop(0, seg // jc, gates, 0)

    def row(j):
        return (pl.ds(pl.multiple_of(j * N_SEG, N_SEG), N_SEG), slice(None))

    def sweep1(j, carry):
        out = []
        for k in range(2 * nb):
            h, p = carry[k]
            jj = j if k < nb else seg - 1 - j
            a = a_ref.at[k][row(jj)]
            out.append((a * h + u_ref.at[k][row(jj)], a * p))
        return tuple(out)

    zero = jnp.zeros((N_SEG, LANES), f32)
    one = jnp.ones((N_SEG, LANES), f32)
    ends = lax.fori_loop(0, seg, sweep1, tuple((zero, one) for _ in range(2 * nb)),
                         unroll=SWEEP_UNROLL)

    starts = []
    for k in range(2 * nb):
        h_end, p_end = ends[k]
        st = zero
        for _ in range(N_SEG - 1):
            st = _shift_rows(h_end + p_end * st, down=(k < nb))
        starts.append(st)

    def sweep2(j, carry):
        out = []
        for k in range(2 * nb):
            jj = j if k < nb else seg - 1 - j
            h = a_ref.at[k][row(jj)] * carry[k] + u_ref.at[k][row(jj)]
            h_ref.at[k][row(jj)] = h
            out.append(h)
        return tuple(out)

    lax.fori_loop(0, seg, sweep2, tuple(starts), unroll=SWEEP_UNROLL)

    def unscan(j, _):
        for n in range(nb):
            hsum = h_ref.at[n][row(j)] + h_ref.at[nb + n][row(j)]
            x_ref.at[n][pl.ds(SEG_OFF + j, N_SEG, stride=pitch), :] = hsum
        return 0

    lax.fori_loop(0, seg, unscan, 0, unroll=SWEEP_UNROLL)

    for s in range(N_SEG):
        hsum = jnp.concatenate(
            [x_ref[n, pl.ds(SEG_OFF + s * pitch, seg), :] for n in range(nb)], axis=1)
        o_ref[pl.ds(s * seg, seg), :] = _bf16(gy_ref[pl.ds(s * seg, seg), :].astype(f32) * hsum)


def _rglru(lx, gy, cw, cb, wg, bg, lam, batch, seq):
    t = lx.shape[0]
    pitch = seq // N_SEG + SEG_PAD
    row = lambda b: (b, 0)
    scan_buf = pltpu.VMEM((2 * LRU_BLOCKS, seq, LANES), jnp.float32)
    return pl.pallas_call(
        functools.partial(_rglru_kernel, seq=seq),
        grid=(batch,),
        in_specs=[
            pl.BlockSpec((seq, LRU_WIDTH), row), pl.BlockSpec((seq, LRU_WIDTH), row),
            _const_spec(cw.shape), _const_spec(cb.shape), _const_spec(wg.shape),
            _const_spec(bg.shape), _const_spec(lam.shape),
        ],
        out_specs=pl.BlockSpec((seq, LRU_WIDTH), row),
        out_shape=jax.ShapeDtypeStruct((t, LRU_WIDTH), jnp.bfloat16),
        scratch_shapes=[
            pltpu.VMEM((LRU_BLOCKS, SEG_OFF + N_SEG * pitch + SUBLANES, LANES), jnp.float32),
            scan_buf, scan_buf, scan_buf,
        ],
        compiler_params=pltpu.CompilerParams(
            dimension_semantics=("arbitrary",), vmem_limit_bytes=VMEM_LIMIT_BYTES),
        name="rglru",
    )(lx, gy, cw, cb, wg, bg, lam)


def _attn_kernel(qt_ref, k_ref, vt_ref, o_ref):
    seq = k_ref.shape[0]
    nblk = seq // TK_ATTN
    heads = [slice(hd * HEAD_PAD, (hd + 1) * HEAD_PAD) for hd in range(N_HEADS)]
    blocks = [slice(j * TK_ATTN, (j + 1) * TK_ATTN) for j in range(nblk)]
    st = {}
    pt = {}
    mx = {}
    acc = {}
    outs = []
    for it in range(N_HEADS + 2):
        ha, hb, hc = it, it - 1, it - 2
        for j in range(nblk):
            if ha < N_HEADS:
                st[ha, j] = _dot(k_ref[blocks[j], heads[ha]], qt_ref[heads[ha], :])
                bm = st[ha, j].max(axis=0, keepdims=True)
                mx[ha] = bm if j == 0 else jnp.maximum(mx[ha], bm)
            if 0 <= hb < N_HEADS:
                pt[hb, j] = _bf16(jnp.exp2(st.pop((hb, j)) - mx[hb]))
            if 0 <= hc < N_HEADS:
                part = _dot(vt_ref[hc * V_SLOT:(hc + 1) * V_SLOT, blocks[j]], pt.pop((hc, j)))
                acc[hc] = part if j == 0 else acc[hc] + part
        if 0 <= hc < N_HEADS:
            a = acc.pop(hc)
            outs.append(a[:V_HEAD_DIM] / a[V_HEAD_DIM:V_HEAD_DIM + 1])
    o_ref[...] = _bf16(jnp.concatenate(outs, axis=0).T)


def _attention(qt, k, vt, batch, seq):
    t = k.shape[0]
    tq = min(TQ_ATTN, seq)
    per_seq = seq // tq
    return pl.pallas_call(
        _attn_kernel,
        grid=(batch, per_seq),
        in_specs=[
            pl.BlockSpec((None, N_HEADS * HEAD_PAD, tq), lambda b, i: (b, 0, i)),
            pl.BlockSpec((seq, N_HEADS * HEAD_PAD), lambda b, i: (b, 0)),
            pl.BlockSpec((None, N_HEADS * V_SLOT, seq), lambda b, i: (b, 0, 0)),
        ],
        out_specs=pl.BlockSpec((tq, N_HEADS * V_HEAD_DIM), lambda b, i: (b * per_seq + i, 0)),
        out_shape=jax.ShapeDtypeStruct((t, N_HEADS * V_HEAD_DIM), jnp.bfloat16),
        compiler_params=pltpu.CompilerParams(
            dimension_semantics=("arbitrary", "arbitrary"), vmem_limit_bytes=VMEM_LIMIT_BYTES),
        name="attention",
    )(qt, k, vt)


def _merge_ffn_kernel(x_ref, attn_ref, rec_ref, gates_ref, woa_ref, wol_ref, wout_ref,
                      g2_ref, wg_ref, wu_ref, wd_ref, gf_ref, o_ref):
    d = x_ref.shape[1]
    half = x_ref.shape[0] // 2
    rows = [pl.ds(r * half, half) for r in range(2)]
    a = [_dot(attn_ref[rw, :], woa_ref[...]) for rw in rows]
    r = [_dot(rec_ref[rw, :], wol_ref[...]) for rw in rows]
    merged = []
    for i, rw in enumerate(rows):
        gates = gates_ref[rw, :].astype(jnp.float32)
        merged.append(_bf16(gates[:, :d] * a[i] + gates[:, d:] * r[i]))
    x1 = [x_ref[rw, :] + _dot(merged[i], wout_ref[...]) for i, rw in enumerate(rows)]
    h2 = [_bf16(_rms(v, g2_ref[...])) for v in x1]
    gate = [_dot(v, wg_ref[...]) for v in h2]
    up = [_dot(v, wu_ref[...]) for v in h2]
    act = [_bf16(jax.nn.silu(gate[i]) * up[i]) for i in range(2)]
    for i, rw in enumerate(rows):
        o_ref[rw, :] = _rms(x1[i] + _dot(act[i], wd_ref[...]), gf_ref[...])


def _merge_ffn(x2, attn, rec, gates, woa, wol, wout, g2, wg, wu, wd, gf):
    t, d = x2.shape
    tm = min(TM_OUT, t)
    row = lambda i: (i, 0)
    return pl.pallas_call(
        _merge_ffn_kernel,
        grid=(t // tm,),
        in_specs=[
            pl.BlockSpec((tm, d), row), pl.BlockSpec((tm, attn.shape[1]), row),
            pl.BlockSpec((tm, rec.shape[1]), row), pl.BlockSpec((tm, 2 * d), row),
            _const_spec(woa.shape), _const_spec(wol.shape), _const_spec(wout.shape),
            _const_spec(g2.shape), _const_spec(wg.shape), _const_spec(wu.shape),
            _const_spec(wd.shape), _const_spec(gf.shape),
        ],
        out_specs=pl.BlockSpec((tm, d), row),
        out_shape=jax.ShapeDtypeStruct((t, d), jnp.float32),
        compiler_params=pltpu.CompilerParams(
            dimension_semantics=("arbitrary",), vmem_limit_bytes=VMEM_LIMIT_BYTES),
        name="merge_ffn",
    )(x2, attn, rec, gates, woa, wol, wout, g2, wg, wu, wd, gf)


def _rope_slot_tables(seq):
    pos = jnp.arange(seq, dtype=jnp.float32)
    inv_freq = 1.0 / (ROPE_THETA ** (jnp.arange(0, QK_ROPE_DIM, 2, dtype=jnp.float32) / QK_ROPE_DIM))
    ang = pos[:, None] * inv_freq[None, :]
    cos, sin = jnp.cos(ang), jnp.sin(ang)
    half = QK_ROPE_DIM // 2
    ones = jnp.ones((seq, QK_NOPE_DIM), jnp.float32)
    z_half = jnp.zeros((seq, half), jnp.float32)
    z_nope = jnp.zeros((seq, QK_NOPE_DIM), jnp.float32)
    z_pad = jnp.zeros((seq, HEAD_PAD - QK_DIM), jnp.float32)
    rc = jnp.concatenate([ones, cos, cos, z_pad], axis=1)
    rlo = jnp.concatenate([z_nope, -sin, z_half, z_pad], axis=1)
    rhi = jnp.concatenate([z_nope, z_half, sin, z_pad], axis=1)
    return rc, rlo, rhi


def _head_slots(w, width):
    k = w.shape[0]
    w = w.reshape(k, N_HEADS, width)
    w = jnp.pad(w, ((0, 0), (0, 0), (0, HEAD_PAD - width)))
    return w.reshape(k, N_HEADS * HEAD_PAD)


def kernel(x, norm1_g, w_in, q_a_norm_g, w_q_b, kv_a_norm_g, w_kv_b, w_o_attn,
           conv_w, conv_b, w_rgate, b_rgate, w_igate, b_igate, lru_lambda, w_o_lru,
           w_out, norm2_g, w_ffn_gate, w_ffn_up, w_ffn_down, final_g):
    batch, seq, d = x.shape
    depth = w_in.shape[0]
    bf = jnp.bfloat16
    c_qa = Q_LORA_RANK
    c_kva = c_qa + KV_LORA_RANK
    c_kr = c_kva + QK_ROPE_DIM
    rc, rlo, rhi = _rope_slot_tables(seq)
    assert depth == 1, "the final rmsnorm is fused into the single layer's last kernel"
    l = 0
    x2 = x.reshape(batch * seq, d)

    wi = w_in[l]
    w_kr = jnp.pad(wi[:, c_kva:c_kr], ((0, 0), (QK_NOPE_DIM, HEAD_PAD - QK_DIM)))
    w_in_r = jnp.concatenate([wi[:, :c_kva], w_kr, wi[:, c_kr:]], axis=1).astype(bf)
    wqb_r = _head_slots(w_q_b[l], QK_DIM).astype(bf)
    wkv = w_kv_b[l].reshape(KV_LORA_RANK, N_HEADS, QK_NOPE_DIM + V_HEAD_DIM)
    wk_r = _head_slots(wkv[:, :, :QK_NOPE_DIM].reshape(KV_LORA_RANK, -1), QK_NOPE_DIM)
    wv = wkv[:, :, QK_NOPE_DIM:].reshape(KV_LORA_RANK, -1)
    wkvb_r = jnp.concatenate([wk_r, wv], axis=1).astype(bf)

    qt, k, vt, lx, gy, gates = _in_proj(
        x2, norm1_g[l][None], w_in_r, q_a_norm_g[l][None], wqb_r, kv_a_norm_g[l][None],
        wkvb_r, rc, rlo, rhi, batch, seq)

    wg = (0.5 * jnp.concatenate([w_rgate[l, 0], w_rgate[l, 1], w_igate[l, 0], w_igate[l, 1]], axis=2)).astype(bf)
    bg = 0.5 * jnp.concatenate([b_rgate[l, 0], b_rgate[l, 1], b_igate[l, 0], b_igate[l, 1]], axis=1)[:, None, :]
    rec = _rglru(lx, gy, conv_w[l], conv_b[l][None], wg, bg, lru_lambda[l], batch, seq)

    attn = _attention(qt, k, vt, batch, seq)

    out = _merge_ffn(x2, attn, rec, gates, w_o_attn[l].astype(bf), w_o_lru[l].astype(bf),
                     w_out[l].astype(bf), norm2_g[l][None], w_ffn_gate[l].astype(bf),
                     w_ffn_up[l].astype(bf), w_ffn_down[l].astype(bf), final_g[None])
    return out.reshape(batch, seq, d)
```

```python
import functools

import jax
import jax.numpy as jnp
from jax import lax
from jax.experimental import pallas as pl
from jax.experimental.pallas import tpu as pltpu

N_HEADS = 8
QK_NOPE_DIM = 64
QK_ROPE_DIM = 32
QK_DIM = QK_NOPE_DIM + QK_ROPE_DIM
V_HEAD_DIM = 64
Q_LORA_RANK = 256
KV_LORA_RANK = 128
ROPE_THETA = 10000.0
LRU_WIDTH = 512
LRU_BLOCKS = 4
LRU_BLOCK_DIM = 128
CONV_WIDTH = 4
CONV_LEFT = 2
LRU_C = 8.0
NORM_EPS = 1e-6
LOG2_E = 1.4426950408889634

LANES = 128
SUBLANES = 8
HEAD_PAD = LANES
V_SLOT = 80
VMEM_LIMIT_BYTES = 56 * 1024 * 1024

TM_IN = 512
TQ_ATTN = 512
TK_ATTN = 256
TM_OUT = 512
N_SEG = SUBLANES
SEG_PAD = 4
SWEEP_UNROLL = 8
SEG_OFF = SUBLANES
SCAN_CHUNK = 128


def _sigmoid(x):
    return 0.5 * jnp.tanh(0.5 * x) + 0.5


def _rms(x, g):
    return x * lax.rsqrt(jnp.mean(x * x, axis=-1, keepdims=True) + NORM_EPS) * g


def _bf16(x):
    return x.astype(jnp.bfloat16)


def _dot(a, b):
    return jnp.dot(a, b, preferred_element_type=jnp.float32)


def _const_spec(shape):
    nd = len(shape)
    return pl.BlockSpec(shape, lambda *_: (0,) * nd, pipeline_mode=pl.Buffered(1))


def _rope_slot(t, c, s_lo, s_hi):
    return t * c + pltpu.roll(t, 16, 1) * s_hi + pltpu.roll(t, LANES - 16, 1) * s_lo


def _in_proj_kernel(x_ref, g1_ref, w_in_ref, gq_ref, wqb_ref, gkv_ref, wkvb_ref,
                    rc_ref, rlo_ref, rhi_ref,
                    qt_ref, k_ref, vt_ref, lx_ref, gy_ref, gates_ref, lat_ref):
    c0 = Q_LORA_RANK
    c1 = c0 + KV_LORA_RANK
    c2 = c1 + HEAD_PAD
    c3 = c2 + LRU_WIDTH
    c4 = c3 + LRU_WIDTH

    @pl.when(pl.program_id(0) == 0)
    def _():
        lat_ref[...] = jnp.zeros(lat_ref.shape, lat_ref.dtype)

    q_a = lat_ref[:, :c0]
    kv_a = lat_ref[:, c0:c1]
    k_rope = lat_ref[:, c1:c2]
    rc = rc_ref[...]
    rlo = rlo_ref[...]
    rhi = rhi_ref[...]
    scale = QK_DIM ** -0.5 * LOG2_E
    q = _dot(_bf16(_rms(q_a, gq_ref[...])), wqb_ref[...])
    kv = _dot(_bf16(_rms(kv_a, gkv_ref[...])), wkvb_ref[...])
    k_rope = _rope_slot(k_rope, rc, rlo, rhi)
    vt = _bf16(kv[:, N_HEADS * HEAD_PAD:].T)
    pad_rows = V_SLOT - V_HEAD_DIM
    ones_row = (lax.broadcasted_iota(jnp.int32, (pad_rows, vt.shape[1]), 0) == 0).astype(vt.dtype)
    for hd in range(N_HEADS):
        sl = slice(hd * HEAD_PAD, (hd + 1) * HEAD_PAD)
        qt_ref[sl, :] = _bf16((_rope_slot(q[:, sl], rc, rlo, rhi) * scale).T)
        k_ref[:, sl] = _bf16(kv[:, sl] + k_rope)
        vt_ref[pl.ds(hd * V_SLOT, V_HEAD_DIM), :] = vt[hd * V_HEAD_DIM:(hd + 1) * V_HEAD_DIM]
        vt_ref[pl.ds(hd * V_SLOT + V_HEAD_DIM, pad_rows), :] = ones_row

    half = x_ref.shape[0] // 2
    for r in range(2):
        rows = pl.ds(r * half, half)
        h = _bf16(_rms(x_ref[rows, :], g1_ref[...]))
        proj = _dot(h, w_in_ref[...])
        lat_ref[rows, :] = proj[:, :c2]
        lx_ref[rows, :] = proj[:, c2:c3]
        gy_ref[rows, :] = _bf16(jax.nn.gelu(proj[:, c3:c4]))
        gates_ref[rows, :] = _bf16(jax.nn.sigmoid(proj[:, c4:]))


def _in_proj(x2, g1, w_in_r, gq, wqb_r, gkv, wkvb_r, rc, rlo, rhi, batch, seq):
    t = x2.shape[0]
    d = x2.shape[1]
    tm = min(TM_IN, seq)
    per_seq = seq // tm
    n = t // tm
    cur = lambda i: jnp.minimum(i, n - 1)
    prev = lambda i: jnp.maximum(i - 1, 0)
    row = lambda i: (cur(i), 0)
    prow = lambda i: (prev(i), 0)
    ppos = lambda i: (prev(i) % per_seq, 0)
    pcol = lambda i: (prev(i) // per_seq, 0, prev(i) % per_seq)
    bf = jnp.bfloat16
    return pl.pallas_call(
        _in_proj_kernel,
        grid=(n + 1,),
        in_specs=[
            pl.BlockSpec((tm, d), row),
            _const_spec(g1.shape), _const_spec(w_in_r.shape), _const_spec(gq.shape),
            _const_spec(wqb_r.shape), _const_spec(gkv.shape), _const_spec(wkvb_r.shape),
            pl.BlockSpec((tm, LANES), ppos), pl.BlockSpec((tm, LANES), ppos),
            pl.BlockSpec((tm, LANES), ppos),
        ],
        out_specs=[
            pl.BlockSpec((None, N_HEADS * HEAD_PAD, tm), pcol),
            pl.BlockSpec((tm, N_HEADS * HEAD_PAD), prow),
            pl.BlockSpec((None, N_HEADS * V_SLOT, tm), pcol),
            pl.BlockSpec((tm, LRU_WIDTH), row),
            pl.BlockSpec((tm, LRU_WIDTH), row),
            pl.BlockSpec((tm, 2 * d), row),
        ],
        out_shape=[
            jax.ShapeDtypeStruct((batch, N_HEADS * HEAD_PAD, seq), bf),
            jax.ShapeDtypeStruct((t, N_HEADS * HEAD_PAD), bf),
            jax.ShapeDtypeStruct((batch, N_HEADS * V_SLOT, seq), bf),
            jax.ShapeDtypeStruct((t, LRU_WIDTH), jnp.float32),
            jax.ShapeDtypeStruct((t, LRU_WIDTH), bf),
            jax.ShapeDtypeStruct((t, 2 * d), bf),
        ],
        scratch_shapes=[pltpu.VMEM((tm, Q_LORA_RANK + KV_LORA_RANK + HEAD_PAD), jnp.float32)],
        compiler_params=pltpu.CompilerParams(
            dimension_semantics=("arbitrary",), vmem_limit_bytes=VMEM_LIMIT_BYTES),
        name="in_proj",
    )(x2, g1, w_in_r, gq, wqb_r, gkv, wkvb_r, rc, rlo, rhi)


def _shift_rows(x, down):
    rows = lax.broadcasted_iota(jnp.int32, x.shape, 0)
    if down:
        return jnp.where(rows == 0, 0.0, pltpu.roll(x, 1, 0))
    return jnp.where(rows == x.shape[0] - 1, 0.0, pltpu.roll(x, x.shape[0] - 1, 0))


def _rglru_kernel(lx_ref, gy_ref, cw_ref, cb_ref, wg_ref, lam_ref, o_ref,
                  x_ref, a_ref, u_ref, h_ref, *, seq):
    seg = seq // N_SEG
    pitch = seg + SEG_PAD
    nb = LRU_BLOCKS
    jc = min(SCAN_CHUNK, seg)
    f32 = jnp.float32

    zeros = jnp.zeros((SUBLANES, LANES), f32)
    for n in range(nb):
        ln = slice(n * LANES, (n + 1) * LANES)
        x_ref[n, pl.ds(0, SUBLANES), :] = zeros
        x_ref[n, pl.ds(SEG_OFF + N_SEG * pitch - SEG_PAD, SUBLANES), :] = zeros
        for s in range(N_SEG):
            x_ref[n, pl.ds(SEG_OFF + s * pitch, seg), :] = lx_ref[pl.ds(s * seg, seg), ln]
        for s in range(1, N_SEG):
            x_ref[n, pl.ds(SEG_OFF + s * pitch - CONV_LEFT, CONV_LEFT), :] = (
                lx_ref[pl.ds(s * seg - CONV_LEFT, CONV_LEFT), ln])
            x_ref[n, pl.ds(SEG_OFF + (s - 1) * pitch + seg, 1), :] = lx_ref[pl.ds(s * seg, 1), ln]

    lam = lam_ref[...]
    z = -lam
    softplus = jnp.maximum(z, 0.0) + jnp.log1p(jnp.exp(-jnp.abs(z)))
    half_decay2 = (-0.5 * LRU_C * LOG2_E) * softplus
    cw = 0.5 * cw_ref[...]
    cb = 0.5 * cb_ref[...]

    def gates(c, _):
        j0 = c * jc
        for n in range(nb):
            ln = slice(n * LANES, (n + 1) * LANES)
            xv = [x_ref.at[n][pl.ds(SEG_OFF - CONV_LEFT + j0 + r, N_SEG, stride=pitch), :]
                  for r in range(jc + CONV_WIDTH - 1)]
            hx = cb[:, ln] + sum(
                jnp.concatenate(xv[o:o + jc], axis=0) * cw[o:o + 1, ln] for o in range(CONV_WIDTH))
            lhs = _bf16(hx)
            lhs = jnp.concatenate([lhs, jnp.ones_like(lhs)], axis=1)
            t = jnp.tanh(_dot(lhs, wg_ref[n]))
            rows = pl.ds(pl.multiple_of(j0 * N_SEG, jc * N_SEG), jc * N_SEG)
            for d in range(2):
                hd = half_decay2[d:d + 1, ln]
                a = jnp.exp2(t[:, d * LANES:(d + 1) * LANES] * hd + hd)
                m2 = 1.0 - a * a
                mult = jnp.where(m2 > 0.0, m2 * lax.rsqrt(m2), 0.0)
                a_ref[d * nb + n, rows, :] = a
                u_ref[d * nb + n, rows, :] = (t[:, (2 + d) * LANES:(3 + d) * LANES] + 1.0) * (hx * mult)
        return 0

    lax.fori_loop(0, seg // jc, gates, 0)

    def row(j):
        return (pl.ds(pl.multiple_of(j * N_SEG, N_SEG), N_SEG), slice(None))

    def sweep1(j, carry):
        out = []
        for k in range(2 * nb):
            h, p = carry[k]
            jj = j if k < nb else seg - 1 - j
            a = a_ref.at[k][row(jj)]
            out.append((a * h + u_ref.at[k][row(jj)], a * p))
        return tuple(out)

    zero = jnp.zeros((N_SEG, LANES), f32)
    one = jnp.ones((N_SEG, LANES), f32)
    ends = lax.fori_loop(0, seg, sweep1, tuple((zero, one) for _ in range(2 * nb)),
                         unroll=SWEEP_UNROLL)

    starts = []
    for k in range(2 * nb):
        h_end, p_end = ends[k]
        st = zero
        for _ in range(N_SEG - 1):
            st = _shift_rows(h_end + p_end * st, down=(k < nb))
        starts.append(st)

    def sweep2(j, carry):
        out = []
        for k in range(2 * nb):
            jj = j if k < nb else seg - 1 - j
            h = a_ref.at[k][row(jj)] * carry[k] + u_ref.at[k][row(jj)]
            h_ref.at[k][row(jj)] = h
            out.append(h)
        return tuple(out)

    lax.fori_loop(0, seg, sweep2, tuple(starts), unroll=SWEEP_UNROLL)

    def unscan(j, _):
        for n in range(nb):
            hsum = h_ref.at[n][row(j)] + h_ref.at[nb + n][row(j)]
            x_ref.at[n][pl.ds(SEG_OFF + j, N_SEG, stride=pitch), :] = hsum
        return 0

    lax.fori_loop(0, seg, unscan, 0, unroll=SWEEP_UNROLL)

    for s in range(N_SEG):
        hsum = jnp.concatenate(
            [x_ref[n, pl.ds(SEG_OFF + s * pitch, seg), :] for n in range(nb)], axis=1)
        o_ref[pl.ds(s * seg, seg), :] = _bf16(gy_ref[pl.ds(s * seg, seg), :].astype(f32) * hsum)


def _rglru(lx, gy, cw, cb, wg, lam, batch, seq):
    t = lx.shape[0]
    pitch = seq // N_SEG + SEG_PAD
    row = lambda b: (b, 0)
    scan_buf = pltpu.VMEM((2 * LRU_BLOCKS, seq, LANES), jnp.float32)
    return pl.pallas_call(
        functools.partial(_rglru_kernel, seq=seq),
        grid=(batch,),
        in_specs=[
            pl.BlockSpec((seq, LRU_WIDTH), row), pl.BlockSpec((seq, LRU_WIDTH), row),
            _const_spec(cw.shape), _const_spec(cb.shape), _const_spec(wg.shape),
            _const_spec(lam.shape),
        ],
        out_specs=pl.BlockSpec((seq, LRU_WIDTH), row),
        out_shape=jax.ShapeDtypeStruct((t, LRU_WIDTH), jnp.bfloat16),
        scratch_shapes=[
            pltpu.VMEM((LRU_BLOCKS, SEG_OFF + N_SEG * pitch + SUBLANES, LANES), jnp.float32),
            scan_buf, scan_buf, scan_buf,
        ],
        compiler_params=pltpu.CompilerParams(
            dimension_semantics=("arbitrary",), vmem_limit_bytes=VMEM_LIMIT_BYTES),
        name="rglru",
    )(lx, gy, cw, cb, wg, lam)


def _attn_kernel(qt_ref, k_ref, vt_ref, o_ref):
    seq = k_ref.shape[0]
    nblk = seq // TK_ATTN
    heads = [slice(hd * HEAD_PAD, (hd + 1) * HEAD_PAD) for hd in range(N_HEADS)]
    blocks = [slice(j * TK_ATTN, (j + 1) * TK_ATTN) for j in range(nblk)]
    st = {}
    pt = {}
    mx = {}
    acc = {}
    outs = []
    for it in range(N_HEADS + 2):
        ha, hb, hc = it, it - 1, it - 2
        for j in range(nblk):
            if ha < N_HEADS:
                st[ha, j] = _dot(k_ref[blocks[j], heads[ha]], qt_ref[heads[ha], :])
                bm = st[ha, j].max(axis=0, keepdims=True)
                mx[ha] = bm if j == 0 else jnp.maximum(mx[ha], bm)
            if 0 <= hb < N_HEADS:
                pt[hb, j] = _bf16(jnp.exp2(st.pop((hb, j)) - mx[hb]))
            if 0 <= hc < N_HEADS:
                part = _dot(vt_ref[hc * V_SLOT:(hc + 1) * V_SLOT, blocks[j]], pt.pop((hc, j)))
                acc[hc] = part if j == 0 else acc[hc] + part
        if 0 <= hc < N_HEADS:
            a = acc.pop(hc)
            outs.append(a[:V_HEAD_DIM] / a[V_HEAD_DIM:V_HEAD_DIM + 1])
    o_ref[...] = _bf16(jnp.concatenate(outs, axis=0).T)


def _attention(qt, k, vt, batch, seq):
    t = k.shape[0]
    tq = min(TQ_ATTN, seq)
    per_seq = seq // tq
    return pl.pallas_call(
        _attn_kernel,
        grid=(batch, per_seq),
        in_specs=[
            pl.BlockSpec((None, N_HEADS * HEAD_PAD, tq), lambda b, i: (b, 0, i)),
            pl.BlockSpec((seq, N_HEADS * HEAD_PAD), lambda b, i: (b, 0)),
            pl.BlockSpec((None, N_HEADS * V_SLOT, seq), lambda b, i: (b, 0, 0)),
        ],
        out_specs=pl.BlockSpec((tq, N_HEADS * V_HEAD_DIM), lambda b, i: (b * per_seq + i, 0)),
        out_shape=jax.ShapeDtypeStruct((t, N_HEADS * V_HEAD_DIM), jnp.bfloat16),
        compiler_params=pltpu.CompilerParams(
            dimension_semantics=("arbitrary", "arbitrary"), vmem_limit_bytes=VMEM_LIMIT_BYTES),
        name="attention",
    )(qt, k, vt)


def _merge_ffn_kernel(x_ref, attn_ref, rec_ref, gates_ref, woa_ref, wol_ref, wout_ref,
                      g2_ref, wg_ref, wu_ref, wd_ref, gf_ref, o_ref):
    d = x_ref.shape[1]
    half = x_ref.shape[0] // 2
    rows = [pl.ds(r * half, half) for r in range(2)]
    a = [_dot(attn_ref[rw, :], woa_ref[...]) for rw in rows]
    r = [_dot(rec_ref[rw, :], wol_ref[...]) for rw in rows]
    merged = []
    for i, rw in enumerate(rows):
        gates = gates_ref[rw, :].astype(jnp.float32)
        merged.append(_bf16(gates[:, :d] * a[i] + gates[:, d:] * r[i]))
    x1 = [x_ref[rw, :] + _dot(merged[i], wout_ref[...]) for i, rw in enumerate(rows)]
    h2 = [_bf16(_rms(v, g2_ref[...])) for v in x1]
    gate = [_dot(v, wg_ref[...]) for v in h2]
    up = [_dot(v, wu_ref[...]) for v in h2]
    act = [_bf16(jax.nn.silu(gate[i]) * up[i]) for i in range(2)]
    for i, rw in enumerate(rows):
        o_ref[rw, :] = _rms(x1[i] + _dot(act[i], wd_ref[...]), gf_ref[...])


def _merge_ffn(x2, attn, rec, gates, woa, wol, wout, g2, wg, wu, wd, gf):
    t, d = x2.shape
    tm = min(TM_OUT, t)
    row = lambda i: (i, 0)
    return pl.pallas_call(
        _merge_ffn_kernel,
        grid=(t // tm,),
        in_specs=[
            pl.BlockSpec((tm, d), row), pl.BlockSpec((tm, attn.shape[1]), row),
            pl.BlockSpec((tm, rec.shape[1]), row), pl.BlockSpec((tm, 2 * d), row),
            _const_spec(woa.shape), _const_spec(wol.shape), _const_spec(wout.shape),
            _const_spec(g2.shape), _const_spec(wg.shape), _const_spec(wu.shape),
            _const_spec(wd.shape), _const_spec(gf.shape),
        ],
        out_specs=pl.BlockSpec((tm, d), row),
        out_shape=jax.ShapeDtypeStruct((t, d), jnp.float32),
        compiler_params=pltpu.CompilerParams(
            dimension_semantics=("arbitrary",), vmem_limit_bytes=VMEM_LIMIT_BYTES),
        name="merge_ffn",
    )(x2, attn, rec, gates, woa, wol, wout, g2, wg, wu, wd, gf)


def _rope_slot_tables(seq):
    pos = jnp.arange(seq, dtype=jnp.float32)
    inv_freq = 1.0 / (ROPE_THETA ** (jnp.arange(0, QK_ROPE_DIM, 2, dtype=jnp.float32) / QK_ROPE_DIM))
    ang = pos[:, None] * inv_freq[None, :]
    cos, sin = jnp.cos(ang), jnp.sin(ang)
    half = QK_ROPE_DIM // 2
    ones = jnp.ones((seq, QK_NOPE_DIM), jnp.float32)
    z_half = jnp.zeros((seq, half), jnp.float32)
    z_nope = jnp.zeros((seq, QK_NOPE_DIM), jnp.float32)
    z_pad = jnp.zeros((seq, HEAD_PAD - QK_DIM), jnp.float32)
    rc = jnp.concatenate([ones, cos, cos, z_pad], axis=1)
    rlo = jnp.concatenate([z_nope, -sin, z_half, z_pad], axis=1)
    rhi = jnp.concatenate([z_nope, z_half, sin, z_pad], axis=1)
    return rc, rlo, rhi


def _head_slots(w, width):
    k = w.shape[0]
    w = w.reshape(k, N_HEADS, width)
    w = jnp.pad(w, ((0, 0), (0, 0), (0, HEAD_PAD - width)))
    return w.reshape(k, N_HEADS * HEAD_PAD)


def kernel(x, norm1_g, w_in, q_a_norm_g, w_q_b, kv_a_norm_g, w_kv_b, w_o_attn,
           conv_w, conv_b, w_rgate, b_rgate, w_igate, b_igate, lru_lambda, w_o_lru,
           w_out, norm2_g, w_ffn_gate, w_ffn_up, w_ffn_down, final_g):
    batch, seq, d = x.shape
    depth = w_in.shape[0]
    bf = jnp.bfloat16
    c_qa = Q_LORA_RANK
    c_kva = c_qa + KV_LORA_RANK
    c_kr = c_kva + QK_ROPE_DIM
    rc, rlo, rhi = _rope_slot_tables(seq)
    assert depth == 1, "the final rmsnorm is fused into the single layer's last kernel"
    l = 0
    x2 = x.reshape(batch * seq, d)

    wi = w_in[l]
    w_kr = jnp.pad(wi[:, c_kva:c_kr], ((0, 0), (QK_NOPE_DIM, HEAD_PAD - QK_DIM)))
    w_in_r = jnp.concatenate([wi[:, :c_kva], w_kr, wi[:, c_kr:]], axis=1).astype(bf)
    wqb_r = _head_slots(w_q_b[l], QK_DIM).astype(bf)
    wkv = w_kv_b[l].reshape(KV_LORA_RANK, N_HEADS, QK_NOPE_DIM + V_HEAD_DIM)
    wk_r = _head_slots(wkv[:, :, :QK_NOPE_DIM].reshape(KV_LORA_RANK, -1), QK_NOPE_DIM)
    wv = wkv[:, :, QK_NOPE_DIM:].reshape(KV_LORA_RANK, -1)
    wkvb_r = jnp.concatenate([wk_r, wv], axis=1).astype(bf)

    qt, k, vt, lx, gy, gates = _in_proj(
        x2, norm1_g[l][None], w_in_r, q_a_norm_g[l][None], wqb_r, kv_a_norm_g[l][None],
        wkvb_r, rc, rlo, rhi, batch, seq)

    wg = jnp.concatenate([w_rgate[l, 0], w_rgate[l, 1], w_igate[l, 0], w_igate[l, 1]], axis=2).astype(bf)
    bg = 0.5 * jnp.concatenate([b_rgate[l, 0], b_rgate[l, 1], b_igate[l, 0], b_igate[l, 1]], axis=1)[:, None, :]
    bg_hi = bg.astype(bf)
    bg_lo = (bg - bg_hi.astype(jnp.float32)).astype(bf)
    wg = jnp.concatenate([wg, bg_hi, bg_lo, jnp.zeros((LRU_BLOCKS, LRU_BLOCK_DIM - 2, wg.shape[2]), bf)], axis=1)
    rec = _rglru(lx, gy, conv_w[l], conv_b[l][None], wg, lru_lambda[l], batch, seq)

    attn = _attention(qt, k, vt, batch, seq)

    out = _merge_ffn(x2, attn, rec, gates, w_o_attn[l].astype(bf), w_o_lru[l].astype(bf),
                     w_out[l].astype(bf), norm2_g[l][None], w_ffn_gate[l].astype(bf),
                     w_ffn_up[l].astype(bf), w_ffn_down[l].astype(bf), final_g[None])
    return out.reshape(batch, seq, d)
```

```python
import functools

import jax
import jax.numpy as jnp
from jax import lax
from jax.experimental import pallas as pl
from jax.experimental.pallas import tpu as pltpu

N_HEADS = 8
QK_NOPE_DIM = 64
QK_ROPE_DIM = 32
QK_DIM = QK_NOPE_DIM + QK_ROPE_DIM
V_HEAD_DIM = 64
Q_LORA_RANK = 256
KV_LORA_RANK = 128
ROPE_THETA = 10000.0
LRU_WIDTH = 512
LRU_BLOCKS = 4
LRU_BLOCK_DIM = 128
CONV_WIDTH = 4
CONV_LEFT = 2
LRU_C = 8.0
NORM_EPS = 1e-6
LOG2_E = 1.4426950408889634

LANES = 128
SUBLANES = 8
HEAD_PAD = LANES
V_SLOT = 80
VMEM_LIMIT_BYTES = 56 * 1024 * 1024

TM_IN = 512
TQ_ATTN = 512
TK_ATTN = 256
TM_OUT = 512
N_SEG = 2 * SUBLANES
SEG_PAD = 4
SWEEP_UNROLL = 8
SEG_OFF = SUBLANES
SCAN_CHUNK = 64


def _sigmoid(x):
    return 0.5 * jnp.tanh(0.5 * x) + 0.5


def _rms(x, g):
    return x * lax.rsqrt(jnp.mean(x * x, axis=-1, keepdims=True) + NORM_EPS) * g


def _bf16(x):
    return x.astype(jnp.bfloat16)


def _dot(a, b):
    return jnp.dot(a, b, preferred_element_type=jnp.float32)


def _const_spec(shape):
    nd = len(shape)
    return pl.BlockSpec(shape, lambda *_: (0,) * nd, pipeline_mode=pl.Buffered(1))


def _rope_slot(t, c, s_lo, s_hi):
    return t * c + pltpu.roll(t, 16, 1) * s_hi + pltpu.roll(t, LANES - 16, 1) * s_lo


def _in_proj_kernel(x_ref, g1_ref, w_in_ref, gq_ref, wqb_ref, gkv_ref, wkvb_ref,
                    rope_ref,
                    qt_ref, k_ref, vt_ref, lx_ref, gy_ref, gates_ref, lat_ref):
    c0 = Q_LORA_RANK
    c1 = c0 + KV_LORA_RANK
    c2 = c1 + HEAD_PAD
    c3 = c2 + LRU_WIDTH
    c4 = c3 + LRU_WIDTH

    @pl.when(pl.program_id(0) == 0)
    def _():
        lat_ref[...] = jnp.zeros(lat_ref.shape, lat_ref.dtype)

    q_a = lat_ref[:, :c0]
    kv_a = lat_ref[:, c0:c1]
    k_rope = lat_ref[:, c1:c2]
    rc = rope_ref[:, :LANES]
    rlo = rope_ref[:, LANES:2 * LANES]
    rhi = rope_ref[:, 2 * LANES:]
    scale = QK_DIM ** -0.5 * LOG2_E
    q = _dot(_bf16(_rms(q_a, gq_ref[...])), wqb_ref[...])
    kv = _dot(_bf16(_rms(kv_a, gkv_ref[...])), wkvb_ref[...])
    k_rope = _rope_slot(k_rope, rc, rlo, rhi)
    vt = _bf16(kv[:, N_HEADS * HEAD_PAD:].T)
    pad_rows = V_SLOT - V_HEAD_DIM
    ones_row = (lax.broadcasted_iota(jnp.int32, (pad_rows, vt.shape[1]), 0) == 0).astype(vt.dtype)
    for hd in range(N_HEADS):
        sl = slice(hd * HEAD_PAD, (hd + 1) * HEAD_PAD)
        qt_ref[sl, :] = _bf16((_rope_slot(q[:, sl], rc, rlo, rhi) * scale).T)
        k_ref[:, sl] = _bf16(kv[:, sl] + k_rope)
        vt_ref[pl.ds(hd * V_SLOT, V_HEAD_DIM), :] = vt[hd * V_HEAD_DIM:(hd + 1) * V_HEAD_DIM]
        vt_ref[pl.ds(hd * V_SLOT + V_HEAD_DIM, pad_rows), :] = ones_row

    half = x_ref.shape[0] // 2
    for r in range(2):
        rows = pl.ds(r * half, half)
        h = _bf16(_rms(x_ref[rows, :], g1_ref[...]))
        proj = _dot(h, w_in_ref[...])
        lat_ref[rows, :] = proj[:, :c2]
        lx_ref[rows, :] = proj[:, c2:c3]
        gy_ref[rows, :] = _bf16(jax.nn.gelu(proj[:, c3:c4]))
        gates_ref[rows, :] = _bf16(jax.nn.sigmoid(proj[:, c4:]))


def _in_proj(x2, g1, w_in_r, gq, wqb_r, gkv, wkvb_r, rope, batch, seq):
    t = x2.shape[0]
    d = x2.shape[1]
    tm = min(TM_IN, seq)
    per_seq = seq // tm
    n = t // tm
    cur = lambda i: jnp.minimum(i, n - 1)
    prev = lambda i: jnp.maximum(i - 1, 0)
    row = lambda i: (cur(i), 0)
    prow = lambda i: (prev(i), 0)
    ppos = lambda i: (prev(i) % per_seq, 0)
    pcol = lambda i: (prev(i) // per_seq, 0, prev(i) % per_seq)
    bf = jnp.bfloat16
    return pl.pallas_call(
        _in_proj_kernel,
        grid=(n + 1,),
        in_specs=[
            pl.BlockSpec((tm, d), row),
            _const_spec(g1.shape), _const_spec(w_in_r.shape), _const_spec(gq.shape),
            _const_spec(wqb_r.shape), _const_spec(gkv.shape), _const_spec(wkvb_r.shape),
            pl.BlockSpec((tm, rope.shape[1]), ppos),
        ],
        out_specs=[
            pl.BlockSpec((None, N_HEADS * HEAD_PAD, tm), pcol),
            pl.BlockSpec((tm, N_HEADS * HEAD_PAD), prow),
            pl.BlockSpec((None, N_HEADS * V_SLOT, tm), pcol),
            pl.BlockSpec((tm, LRU_WIDTH), row),
            pl.BlockSpec((tm, LRU_WIDTH), row),
            pl.BlockSpec((tm, 2 * d), row),
        ],
        out_shape=[
            jax.ShapeDtypeStruct((batch, N_HEADS * HEAD_PAD, seq), bf),
            jax.ShapeDtypeStruct((t, N_HEADS * HEAD_PAD), bf),
            jax.ShapeDtypeStruct((batch, N_HEADS * V_SLOT, seq), bf),
            jax.ShapeDtypeStruct((t, LRU_WIDTH), jnp.float32),
            jax.ShapeDtypeStruct((t, LRU_WIDTH), bf),
            jax.ShapeDtypeStruct((t, 2 * d), bf),
        ],
        scratch_shapes=[pltpu.VMEM((tm, Q_LORA_RANK + KV_LORA_RANK + HEAD_PAD), jnp.float32)],
        compiler_params=pltpu.CompilerParams(
            dimension_semantics=("arbitrary",), vmem_limit_bytes=VMEM_LIMIT_BYTES),
        name="in_proj",
    )(x2, g1, w_in_r, gq, wqb_r, gkv, wkvb_r, rope)


def _shift_rows(x, down):
    rows = lax.broadcasted_iota(jnp.int32, x.shape, 0)
    if down:
        return jnp.where(rows == 0, 0.0, pltpu.roll(x, 1, 0))
    return jnp.where(rows == x.shape[0] - 1, 0.0, pltpu.roll(x, x.shape[0] - 1, 0))


def _rglru_kernel(lx_ref, gy_ref, cw_ref, cb_ref, wg_ref, lam_ref, o_ref,
                  x_ref, a_ref, u_ref, h_ref, *, seq):
    seg = seq // N_SEG
    pitch = seg + SEG_PAD
    nb = LRU_BLOCKS
    jc = min(SCAN_CHUNK, seg)
    f32 = jnp.float32

    zeros = jnp.zeros((SUBLANES, LANES), f32)
    for n in range(nb):
        ln = slice(n * LANES, (n + 1) * LANES)
        x_ref[n, pl.ds(0, SUBLANES), :] = zeros
        x_ref[n, pl.ds(SEG_OFF + N_SEG * pitch - SEG_PAD, SUBLANES), :] = zeros
        for s in range(N_SEG):
            x_ref[n, pl.ds(SEG_OFF + s * pitch, seg), :] = lx_ref[pl.ds(s * seg, seg), ln]
        for s in range(1, N_SEG):
            x_ref[n, pl.ds(SEG_OFF + s * pitch - CONV_LEFT, CONV_LEFT), :] = (
                lx_ref[pl.ds(s * seg - CONV_LEFT, CONV_LEFT), ln])
            x_ref[n, pl.ds(SEG_OFF + (s - 1) * pitch + seg, 1), :] = lx_ref[pl.ds(s * seg, 1), ln]

    lam = lam_ref[...]
    z = -lam
    softplus = jnp.maximum(z, 0.0) + jnp.log1p(jnp.exp(-jnp.abs(z)))
    half_decay2 = (-0.5 * LRU_C * LOG2_E) * softplus
    cw = 0.5 * cw_ref[...]
    cb = 0.5 * cb_ref[...]

    def gates(c, _):
        j0 = c * jc
        for n in range(nb):
            ln = slice(n * LANES, (n + 1) * LANES)
            xv = [x_ref.at[n][pl.ds(SEG_OFF - CONV_LEFT + j0 + r, N_SEG, stride=pitch), :]
                  for r in range(jc + CONV_WIDTH - 1)]
            hx = cb[:, ln] + sum(
                jnp.concatenate(xv[o:o + jc], axis=0) * cw[o:o + 1, ln] for o in range(CONV_WIDTH))
            lhs = _bf16(hx)
            lhs = jnp.concatenate([lhs, jnp.ones_like(lhs)], axis=1)
            t = jnp.tanh(_dot(lhs, wg_ref[n]))
            rows = pl.ds(pl.multiple_of(j0 * N_SEG, jc * N_SEG), jc * N_SEG)
            for d in range(2):
                hd = half_decay2[d:d + 1, ln]
                a = jnp.exp2(t[:, d * LANES:(d + 1) * LANES] * hd + hd)
                m2 = 1.0 - a * a
                mult = jnp.where(m2 > 0.0, m2 * lax.rsqrt(m2), 0.0)
                a_ref[d * nb + n, rows, :] = a
                u_ref[d * nb + n, rows, :] = (t[:, (2 + d) * LANES:(3 + d) * LANES] + 1.0) * (hx * mult)
        return 0

    lax.fori_loop(0, seg // jc, gates, 0)

    def row(j):
        return (pl.ds(pl.multiple_of(j * N_SEG, N_SEG), N_SEG), slice(None))

    def sweep1(j, carry):
        out = []
        for k in range(2 * nb):
            h, p = carry[k]
            jj = j if k < nb else seg - 1 - j
            a = a_ref.at[k][row(jj)]
            out.append((a * h + u_ref.at[k][row(jj)], a * p))
        return tuple(out)

    zero = jnp.zeros((N_SEG, LANES), f32)
    one = jnp.ones((N_SEG, LANES), f32)
    ends = lax.fori_loop(0, seg, sweep1, tuple((zero, one) for _ in range(2 * nb)),
                         unroll=SWEEP_UNROLL)

    starts = []
    for k in range(2 * nb):
        h_end, p_end = ends[k]
        st = zero
        for _ in range(N_SEG - 1):
            st = _shift_rows(h_end + p_end * st, down=(k < nb))
        starts.append(st)

    def sweep2(j, carry):
        out = []
        for k in range(2 * nb):
            jj = j if k < nb else seg - 1 - j
            h = a_ref.at[k][row(jj)] * carry[k] + u_ref.at[k][row(jj)]
            h_ref.at[k][row(jj)] = h
            out.append(h)
        return tuple(out)

    lax.fori_loop(0, seg, sweep2, tuple(starts), unroll=SWEEP_UNROLL)

    def unscan(j, _):
        for n in range(nb):
            hsum = h_ref.at[n][row(j)] + h_ref.at[nb + n][row(j)]
            x_ref.at[n][pl.ds(SEG_OFF + j, N_SEG, stride=pitch), :] = hsum
        return 0

    lax.fori_loop(0, seg, unscan, 0, unroll=SWEEP_UNROLL)

    for s in range(N_SEG):
        hsum = jnp.concatenate(
            [x_ref[n, pl.ds(SEG_OFF + s * pitch, seg), :] for n in range(nb)], axis=1)
        o_ref[pl.ds(s * seg, seg), :] = _bf16(gy_ref[pl.ds(s * seg, seg), :].astype(f32) * hsum)


def _rglru(lx, gy, cw, cb, wg, lam, batch, seq):
    t = lx.shape[0]
    pitch = seq // N_SEG + SEG_PAD
    row = lambda b: (b, 0)
    scan_buf = pltpu.VMEM((2 * LRU_BLOCKS, seq, LANES), jnp.float32)
    return pl.pallas_call(
        functools.partial(_rglru_kernel, seq=seq),
        grid=(batch,),
        in_specs=[
            pl.BlockSpec((seq, LRU_WIDTH), row), pl.BlockSpec((seq, LRU_WIDTH), row),
            _const_spec(cw.shape), _const_spec(cb.shape), _const_spec(wg.shape),
            _const_spec(lam.shape),
        ],
        out_specs=pl.BlockSpec((seq, LRU_WIDTH), row),
        out_shape=jax.ShapeDtypeStruct((t, LRU_WIDTH), jnp.bfloat16),
        scratch_shapes=[
            pltpu.VMEM((LRU_BLOCKS, SEG_OFF + N_SEG * pitch + SUBLANES, LANES), jnp.float32),
            scan_buf, scan_buf, scan_buf,
        ],
        compiler_params=pltpu.CompilerParams(
            dimension_semantics=("arbitrary",), vmem_limit_bytes=VMEM_LIMIT_BYTES),
        name="rglru",
    )(lx, gy, cw, cb, wg, lam)


def _attn_kernel(qt_ref, k_ref, vt_ref, o_ref):
    seq = k_ref.shape[0]
    nblk = seq // TK_ATTN
    heads = [slice(hd * HEAD_PAD, (hd + 1) * HEAD_PAD) for hd in range(N_HEADS)]
    blocks = [slice(j * TK_ATTN, (j + 1) * TK_ATTN) for j in range(nblk)]
    st = {}
    pt = {}
    mx = {}
    acc = {}
    outs = []
    for it in range(N_HEADS + 2):
        ha, hb, hc = it, it - 1, it - 2
        for j in range(nblk):
            if ha < N_HEADS:
                st[ha, j] = _dot(k_ref[blocks[j], heads[ha]], qt_ref[heads[ha], :])
                bm = st[ha, j].max(axis=0, keepdims=True)
                mx[ha] = bm if j == 0 else jnp.maximum(mx[ha], bm)
            if 0 <= hb < N_HEADS:
                pt[hb, j] = _bf16(jnp.exp2(st.pop((hb, j)) - mx[hb]))
            if 0 <= hc < N_HEADS:
                part = _dot(vt_ref[hc * V_SLOT:(hc + 1) * V_SLOT, blocks[j]], pt.pop((hc, j)))
                acc[hc] = part if j == 0 else acc[hc] + part
        if 0 <= hc < N_HEADS:
            a = acc.pop(hc)
            outs.append(a[:V_HEAD_DIM] / a[V_HEAD_DIM:V_HEAD_DIM + 1])
    o_ref[...] = _bf16(jnp.concatenate(outs, axis=0).T)


def _attention(qt, k, vt, batch, seq):
    t = k.shape[0]
    tq = min(TQ_ATTN, seq)
    per_seq = seq // tq
    return pl.pallas_call(
        _attn_kernel,
        grid=(batch, per_seq),
        in_specs=[
            pl.BlockSpec((None, N_HEADS * HEAD_PAD, tq), lambda b, i: (b, 0, i)),
            pl.BlockSpec((seq, N_HEADS * HEAD_PAD), lambda b, i: (b, 0)),
            pl.BlockSpec((None, N_HEADS * V_SLOT, seq), lambda b, i: (b, 0, 0)),
        ],
        out_specs=pl.BlockSpec((tq, N_HEADS * V_HEAD_DIM), lambda b, i: (b * per_seq + i, 0)),
        out_shape=jax.ShapeDtypeStruct((t, N_HEADS * V_HEAD_DIM), jnp.bfloat16),
        compiler_params=pltpu.CompilerParams(
            dimension_semantics=("arbitrary", "arbitrary"), vmem_limit_bytes=VMEM_LIMIT_BYTES),
        name="attention",
    )(qt, k, vt)


def _merge_ffn_kernel(x_ref, attn_ref, rec_ref, gates_ref, woa_ref, wol_ref, wout_ref,
                      g2_ref, wg_ref, wu_ref, wd_ref, gf_ref, o_ref):
    d = x_ref.shape[1]
    half = x_ref.shape[0] // 2
    rows = [pl.ds(r * half, half) for r in range(2)]
    a = [_dot(attn_ref[rw, :], woa_ref[...]) for rw in rows]
    r = [_dot(rec_ref[rw, :], wol_ref[...]) for rw in rows]
    merged = []
    for i, rw in enumerate(rows):
        gates = gates_ref[rw, :].astype(jnp.float32)
        merged.append(_bf16(gates[:, :d] * a[i] + gates[:, d:] * r[i]))
    x1 = [x_ref[rw, :] + _dot(merged[i], wout_ref[...]) for i, rw in enumerate(rows)]
    h2 = [_bf16(_rms(v, g2_ref[...])) for v in x1]
    gate = [_dot(v, wg_ref[...]) for v in h2]
    up = [_dot(v, wu_ref[...]) for v in h2]
    act = [_bf16(jax.nn.silu(gate[i]) * up[i]) for i in range(2)]
    for i, rw in enumerate(rows):
        o_ref[rw, :] = _rms(x1[i] + _dot(act[i], wd_ref[...]), gf_ref[...])


def _merge_ffn(x2, attn, rec, gates, woa, wol, wout, g2, wg, wu, wd, gf):
    t, d = x2.shape
    tm = min(TM_OUT, t)
    row = lambda i: (i, 0)
    return pl.pallas_call(
        _merge_ffn_kernel,
        grid=(t // tm,),
        in_specs=[
            pl.BlockSpec((tm, d), row), pl.BlockSpec((tm, attn.shape[1]), row),
            pl.BlockSpec((tm, rec.shape[1]), row), pl.BlockSpec((tm, 2 * d), row),
            _const_spec(woa.shape), _const_spec(wol.shape), _const_spec(wout.shape),
            _const_spec(g2.shape), _const_spec(wg.shape), _const_spec(wu.shape),
            _const_spec(wd.shape), _const_spec(gf.shape),
        ],
        out_specs=pl.BlockSpec((tm, d), row),
        out_shape=jax.ShapeDtypeStruct((t, d), jnp.float32),
        compiler_params=pltpu.CompilerParams(
            dimension_semantics=("arbitrary",), vmem_limit_bytes=VMEM_LIMIT_BYTES),
        name="merge_ffn",
    )(x2, attn, rec, gates, woa, wol, wout, g2, wg, wu, wd, gf)


def _rope_slot_tables(seq):
    pos = jnp.arange(seq, dtype=jnp.float32)
    inv_freq = 1.0 / (ROPE_THETA ** (jnp.arange(0, QK_ROPE_DIM, 2, dtype=jnp.float32) / QK_ROPE_DIM))
    ang = pos[:, None] * inv_freq[None, :]
    cos, sin = jnp.cos(ang), jnp.sin(ang)
    half = QK_ROPE_DIM // 2
    ones = jnp.ones((seq, QK_NOPE_DIM), jnp.float32)
    z_half = jnp.zeros((seq, half), jnp.float32)
    z_nope = jnp.zeros((seq, QK_NOPE_DIM), jnp.float32)
    z_pad = jnp.zeros((seq, HEAD_PAD - QK_DIM), jnp.float32)
    rc = jnp.concatenate([ones, cos, cos, z_pad], axis=1)
    rlo = jnp.concatenate([z_nope, -sin, z_half, z_pad], axis=1)
    rhi = jnp.concatenate([z_nope, z_half, sin, z_pad], axis=1)
    return jnp.concatenate([rc, rlo, rhi], axis=1)


def _head_slots(w, width):
    k = w.shape[0]
    w = w.reshape(k, N_HEADS, width)
    w = jnp.pad(w, ((0, 0), (0, 0), (0, HEAD_PAD - width)))
    return w.reshape(k, N_HEADS * HEAD_PAD)


def kernel(x, norm1_g, w_in, q_a_norm_g, w_q_b, kv_a_norm_g, w_kv_b, w_o_attn,
           conv_w, conv_b, w_rgate, b_rgate, w_igate, b_igate, lru_lambda, w_o_lru,
           w_out, norm2_g, w_ffn_gate, w_ffn_up, w_ffn_down, final_g):
    batch, seq, d = x.shape
    depth = w_in.shape[0]
    bf = jnp.bfloat16
    c_qa = Q_LORA_RANK
    c_kva = c_qa + KV_LORA_RANK
    c_kr = c_kva + QK_ROPE_DIM
    rope = _rope_slot_tables(seq)
    assert depth == 1, "the final rmsnorm is fused into the single layer's last kernel"
    l = 0
    x2 = x.reshape(batch * seq, d)

    wi = w_in[l]
    w_kr = jnp.pad(wi[:, c_kva:c_kr], ((0, 0), (QK_NOPE_DIM, HEAD_PAD - QK_DIM)))
    w_in_r = jnp.concatenate([wi[:, :c_kva], w_kr, wi[:, c_kr:]], axis=1).astype(bf)
    wqb_r = _head_slots(w_q_b[l], QK_DIM).astype(bf)
    wkv = w_kv_b[l].reshape(KV_LORA_RANK, N_HEADS, QK_NOPE_DIM + V_HEAD_DIM)
    wk_r = _head_slots(wkv[:, :, :QK_NOPE_DIM].reshape(KV_LORA_RANK, -1), QK_NOPE_DIM)
    wv = wkv[:, :, QK_NOPE_DIM:].reshape(KV_LORA_RANK, -1)
    wkvb_r = jnp.concatenate([wk_r, wv], axis=1).astype(bf)

    qt, k, vt, lx, gy, gates = _in_proj(
        x2, norm1_g[l][None], w_in_r, q_a_norm_g[l][None], wqb_r, kv_a_norm_g[l][None],
        wkvb_r, rope, batch, seq)

    wg = jnp.concatenate([w_rgate[l, 0], w_rgate[l, 1], w_igate[l, 0], w_igate[l, 1]], axis=2).astype(bf)
    bg = 0.5 * jnp.concatenate([b_rgate[l, 0], b_rgate[l, 1], b_igate[l, 0], b_igate[l, 1]], axis=1)[:, None, :]
    bg_hi = bg.astype(bf)
    bg_lo = (bg - bg_hi.astype(jnp.float32)).astype(bf)
    wg = jnp.concatenate([wg, bg_hi, bg_lo, jnp.zeros((LRU_BLOCKS, LRU_BLOCK_DIM - 2, wg.shape[2]), bf)], axis=1)
    rec = _rglru(lx, gy, conv_w[l], conv_b[l][None], wg, lru_lambda[l], batch, seq)

    attn = _attention(qt, k, vt, batch, seq)

    out = _merge_ffn(x2, attn, rec, gates, w_o_attn[l].astype(bf), w_o_lru[l].astype(bf),
                     w_out[l].astype(bf), norm2_g[l][None], w_ffn_gate[l].astype(bf),
                     w_ffn_up[l].astype(bf), w_ffn_down[l].astype(bf), final_g[None])
    return out.reshape(batch, seq, d)
```

```python
import functools

import jax
import jax.numpy as jnp
from jax import lax
from jax.experimental import pallas as pl
from jax.experimental.pallas import tpu as pltpu

N_HEADS = 8
QK_NOPE_DIM = 64
QK_ROPE_DIM = 32
QK_DIM = QK_NOPE_DIM + QK_ROPE_DIM
V_HEAD_DIM = 64
Q_LORA_RANK = 256
KV_LORA_RANK = 128
ROPE_THETA = 10000.0
LRU_WIDTH = 512
LRU_BLOCKS = 4
LRU_BLOCK_DIM = 128
CONV_WIDTH = 4
CONV_LEFT = 2
LRU_C = 8.0
NORM_EPS = 1e-6
LOG2_E = 1.4426950408889634

LANES = 128
SUBLANES = 8
HEAD_PAD = LANES
V_SLOT = 80
VMEM_LIMIT_BYTES = 56 * 1024 * 1024

TM_IN = 512
TQ_ATTN = 512
TK_ATTN = 256
TM_OUT = 512
N_SEG = 2 * SUBLANES
SEG_PAD = 4
SWEEP_UNROLL = 8
SEG_OFF = SUBLANES
SCAN_CHUNK = 64


def _sigmoid(x):
    return 0.5 * jnp.tanh(0.5 * x) + 0.5


def _rms(x, g):
    return x * lax.rsqrt(jnp.mean(x * x, axis=-1, keepdims=True) + NORM_EPS) * g


def _bf16(x):
    return x.astype(jnp.bfloat16)


def _dot(a, b):
    return jnp.dot(a, b, preferred_element_type=jnp.float32)


def _const_spec(shape):
    nd = len(shape)
    return pl.BlockSpec(shape, lambda *_: (0,) * nd, pipeline_mode=pl.Buffered(1))


def _rope_slot(t, c, s_lo, s_hi):
    return t * c + pltpu.roll(t, 16, 1) * s_hi + pltpu.roll(t, LANES - 16, 1) * s_lo


def _in_proj_kernel(x_ref, g1_ref, w_in_ref, gq_ref, wqb_ref, gkv_ref, wkvb_ref,
                    rope_ref,
                    qt_ref, k_ref, vt_ref, lx_ref, gy_ref, gates_ref, qn_ref, kvn_ref, kr_ref):
    c0 = Q_LORA_RANK
    c1 = c0 + KV_LORA_RANK
    c2 = c1 + HEAD_PAD
    c3 = c2 + LRU_WIDTH
    c4 = c3 + LRU_WIDTH

    @pl.when(pl.program_id(0) == 0)
    def _():
        for ref in (qn_ref, kvn_ref, kr_ref):
            ref[...] = jnp.zeros(ref.shape, ref.dtype)

    rc = rope_ref[:, :LANES]
    rlo = rope_ref[:, LANES:2 * LANES]
    rhi = rope_ref[:, 2 * LANES:]
    scale = QK_DIM ** -0.5 * LOG2_E
    q = _dot(qn_ref[...], wqb_ref[...])
    kv = _dot(kvn_ref[...], wkvb_ref[...])
    k_rope = _rope_slot(kr_ref[...], rc, rlo, rhi)
    vt = _bf16(kv[:, N_HEADS * HEAD_PAD:].T)
    pad_rows = V_SLOT - V_HEAD_DIM
    ones_row = (lax.broadcasted_iota(jnp.int32, (pad_rows, vt.shape[1]), 0) == 0).astype(vt.dtype)
    for hd in range(N_HEADS):
        sl = slice(hd * HEAD_PAD, (hd + 1) * HEAD_PAD)
        qt_ref[sl, :] = _bf16((_rope_slot(q[:, sl], rc, rlo, rhi) * scale).T)
        k_ref[:, sl] = _bf16(kv[:, sl] + k_rope)
        vt_ref[pl.ds(hd * V_SLOT, V_HEAD_DIM), :] = vt[hd * V_HEAD_DIM:(hd + 1) * V_HEAD_DIM]
        vt_ref[pl.ds(hd * V_SLOT + V_HEAD_DIM, pad_rows), :] = ones_row

    half = x_ref.shape[0] // 2
    for r in range(2):
        rows = pl.ds(r * half, half)
        h = _bf16(_rms(x_ref[rows, :], g1_ref[...]))
        proj = _dot(h, w_in_ref[...])
        qn_ref[rows, :] = _bf16(_rms(proj[:, :c0], gq_ref[...]))
        kvn_ref[rows, :] = _bf16(_rms(proj[:, c0:c1], gkv_ref[...]))
        kr_ref[rows, :] = proj[:, c1:c2]
        lx_ref[rows, :] = proj[:, c2:c3]
        gy_ref[rows, :] = _bf16(jax.nn.gelu(proj[:, c3:c4]))
        gates_ref[rows, :] = _bf16(jax.nn.sigmoid(proj[:, c4:]))


def _in_proj(x2, g1, w_in_r, gq, wqb_r, gkv, wkvb_r, rope, batch, seq):
    t = x2.shape[0]
    d = x2.shape[1]
    tm = min(TM_IN, seq)
    per_seq = seq // tm
    n = t // tm
    cur = lambda i: jnp.minimum(i, n - 1)
    prev = lambda i: jnp.maximum(i - 1, 0)
    row = lambda i: (cur(i), 0)
    prow = lambda i: (prev(i), 0)
    ppos = lambda i: (prev(i) % per_seq, 0)
    pcol = lambda i: (prev(i) // per_seq, 0, prev(i) % per_seq)
    bf = jnp.bfloat16
    return pl.pallas_call(
        _in_proj_kernel,
        grid=(n + 1,),
        in_specs=[
            pl.BlockSpec((tm, d), row),
            _const_spec(g1.shape), _const_spec(w_in_r.shape), _const_spec(gq.shape),
            _const_spec(wqb_r.shape), _const_spec(gkv.shape), _const_spec(wkvb_r.shape),
            pl.BlockSpec((tm, rope.shape[1]), ppos),
        ],
        out_specs=[
            pl.BlockSpec((None, N_HEADS * HEAD_PAD, tm), pcol),
            pl.BlockSpec((tm, N_HEADS * HEAD_PAD), prow),
            pl.BlockSpec((None, N_HEADS * V_SLOT, tm), pcol),
            pl.BlockSpec((tm, LRU_WIDTH), row),
            pl.BlockSpec((tm, LRU_WIDTH), row),
            pl.BlockSpec((tm, 2 * d), row),
        ],
        out_shape=[
            jax.ShapeDtypeStruct((batch, N_HEADS * HEAD_PAD, seq), bf),
            jax.ShapeDtypeStruct((t, N_HEADS * HEAD_PAD), bf),
            jax.ShapeDtypeStruct((batch, N_HEADS * V_SLOT, seq), bf),
            jax.ShapeDtypeStruct((t, LRU_WIDTH), jnp.float32),
            jax.ShapeDtypeStruct((t, LRU_WIDTH), bf),
            jax.ShapeDtypeStruct((t, 2 * d), bf),
        ],
        scratch_shapes=[pltpu.VMEM((tm, Q_LORA_RANK), bf), pltpu.VMEM((tm, KV_LORA_RANK), bf),
                        pltpu.VMEM((tm, HEAD_PAD), jnp.float32)],
        compiler_params=pltpu.CompilerParams(
            dimension_semantics=("arbitrary",), vmem_limit_bytes=VMEM_LIMIT_BYTES),
        name="in_proj",
    )(x2, g1, w_in_r, gq, wqb_r, gkv, wkvb_r, rope)


def _shift_rows(x, down):
    rows = lax.broadcasted_iota(jnp.int32, x.shape, 0)
    if down:
        return jnp.where(rows == 0, 0.0, pltpu.roll(x, 1, 0))
    return jnp.where(rows == x.shape[0] - 1, 0.0, pltpu.roll(x, x.shape[0] - 1, 0))


def _rglru_kernel(lx_ref, gy_ref, cw_ref, cb_ref, wg_ref, lam_ref, o_ref,
                  x_ref, a_ref, u_ref, h_ref, *, seq):
    seg = seq // N_SEG
    pitch = seg + SEG_PAD
    nb = LRU_BLOCKS
    jc = min(SCAN_CHUNK, seg)
    f32 = jnp.float32

    zeros = jnp.zeros((SUBLANES, LANES), f32)
    for n in range(nb):
        ln = slice(n * LANES, (n + 1) * LANES)
        x_ref[n, pl.ds(0, SUBLANES), :] = zeros
        x_ref[n, pl.ds(SEG_OFF + N_SEG * pitch - SEG_PAD, SUBLANES), :] = zeros
        for s in range(N_SEG):
            x_ref[n, pl.ds(SEG_OFF + s * pitch, seg), :] = lx_ref[pl.ds(s * seg, seg), ln]
        for s in range(1, N_SEG):
            x_ref[n, pl.ds(SEG_OFF + s * pitch - CONV_LEFT, CONV_LEFT), :] = (
                lx_ref[pl.ds(s * seg - CONV_LEFT, CONV_LEFT), ln])
            x_ref[n, pl.ds(SEG_OFF + (s - 1) * pitch + seg, 1), :] = lx_ref[pl.ds(s * seg, 1), ln]

    lam = lam_ref[...]
    z = -lam
    softplus = jnp.maximum(z, 0.0) + jnp.log1p(jnp.exp(-jnp.abs(z)))
    half_decay2 = (-0.5 * LRU_C * LOG2_E) * softplus
    cw = 0.5 * cw_ref[...]
    cb = 0.5 * cb_ref[...]

    def gates(c, _):
        j0 = c * jc
        for n in range(nb):
            ln = slice(n * LANES, (n + 1) * LANES)
            xv = [x_ref.at[n][pl.ds(SEG_OFF - CONV_LEFT + j0 + r, N_SEG, stride=pitch), :]
                  for r in range(jc + CONV_WIDTH - 1)]
            hx = cb[:, ln] + sum(
                jnp.concatenate(xv[o:o + jc], axis=0) * cw[o:o + 1, ln] for o in range(CONV_WIDTH))
            lhs = _bf16(hx)
            lhs = jnp.concatenate([lhs, jnp.ones_like(lhs)], axis=1)
            t = jnp.tanh(_dot(lhs, wg_ref[n]))
            rows = pl.ds(pl.multiple_of(j0 * N_SEG, jc * N_SEG), jc * N_SEG)
            for d in range(2):
                hd = half_decay2[d:d + 1, ln]
                a = jnp.exp2(t[:, d * LANES:(d + 1) * LANES] * hd + hd)
                m2 = 1.0 - a * a
                mult = jnp.where(m2 > 0.0, m2 * lax.rsqrt(m2), 0.0)
                a_ref[d * nb + n, rows, :] = a
                u_ref[d * nb + n, rows, :] = (t[:, (2 + d) * LANES:(3 + d) * LANES] + 1.0) * (hx * mult)
        return 0

    lax.fori_loop(0, seg // jc, gates, 0)

    def row(j):
        return (pl.ds(pl.multiple_of(j * N_SEG, N_SEG), N_SEG), slice(None))

    def sweep1(j, carry):
        out = []
        for k in range(2 * nb):
            h, p = carry[k]
            jj = j if k < nb else seg - 1 - j
            a = a_ref.at[k][row(jj)]
            out.append((a * h + u_ref.at[k][row(jj)], a * p))
        return tuple(out)

    zero = jnp.zeros((N_SEG, LANES), f32)
    one = jnp.ones((N_SEG, LANES), f32)
    ends = lax.fori_loop(0, seg, sweep1, tuple((zero, one) for _ in range(2 * nb)),
                         unroll=SWEEP_UNROLL)

    starts = []
    for k in range(2 * nb):
        h_end, p_end = ends[k]
        st = zero
        for _ in range(N_SEG - 1):
            st = _shift_rows(h_end + p_end * st, down=(k < nb))
        starts.append(st)

    def sweep2(j, carry):
        out = []
        for k in range(2 * nb):
            jj = j if k < nb else seg - 1 - j
            h = a_ref.at[k][row(jj)] * carry[k] + u_ref.at[k][row(jj)]
            h_ref.at[k][row(jj)] = h
            out.append(h)
        return tuple(out)

    lax.fori_loop(0, seg, sweep2, tuple(starts), unroll=SWEEP_UNROLL)

    def unscan(j, _):
        for n in range(nb):
            hsum = h_ref.at[n][row(j)] + h_ref.at[nb + n][row(j)]
            x_ref.at[n][pl.ds(SEG_OFF + j, N_SEG, stride=pitch), :] = hsum
        return 0

    lax.fori_loop(0, seg, unscan, 0, unroll=SWEEP_UNROLL)

    for s in range(N_SEG):
        hsum = jnp.concatenate(
            [x_ref[n, pl.ds(SEG_OFF + s * pitch, seg), :] for n in range(nb)], axis=1)
        o_ref[pl.ds(s * seg, seg), :] = _bf16(gy_ref[pl.ds(s * seg, seg), :].astype(f32) * hsum)


def _rglru(lx, gy, cw, cb, wg, lam, batch, seq):
    t = lx.shape[0]
    pitch = seq // N_SEG + SEG_PAD
    row = lambda b: (b, 0)
    scan_buf = pltpu.VMEM((2 * LRU_BLOCKS, seq, LANES), jnp.float32)
    return pl.pallas_call(
        functools.partial(_rglru_kernel, seq=seq),
        grid=(batch,),
        in_specs=[
            pl.BlockSpec((seq, LRU_WIDTH), row), pl.BlockSpec((seq, LRU_WIDTH), row),
            _const_spec(cw.shape), _const_spec(cb.shape), _const_spec(wg.shape),
            _const_spec(lam.shape),
        ],
        out_specs=pl.BlockSpec((seq, LRU_WIDTH), row),
        out_shape=jax.ShapeDtypeStruct((t, LRU_WIDTH), jnp.bfloat16),
        scratch_shapes=[
            pltpu.VMEM((LRU_BLOCKS, SEG_OFF + N_SEG * pitch + SUBLANES, LANES), jnp.float32),
            scan_buf, scan_buf, scan_buf,
        ],
        compiler_params=pltpu.CompilerParams(
            dimension_semantics=("arbitrary",), vmem_limit_bytes=VMEM_LIMIT_BYTES),
        name="rglru",
    )(lx, gy, cw, cb, wg, lam)


def _attn_kernel(qt_ref, k_ref, vt_ref, o_ref):
    seq = k_ref.shape[0]
    nblk = seq // TK_ATTN
    heads = [slice(hd * HEAD_PAD, (hd + 1) * HEAD_PAD) for hd in range(N_HEADS)]
    blocks = [slice(j * TK_ATTN, (j + 1) * TK_ATTN) for j in range(nblk)]
    st = {}
    pt = {}
    mx = {}
    acc = {}
    outs = []
    for it in range(N_HEADS + 2):
        ha, hb, hc = it, it - 1, it - 2
        for j in range(nblk):
            if ha < N_HEADS:
                st[ha, j] = _dot(k_ref[blocks[j], heads[ha]], qt_ref[heads[ha], :])
                bm = st[ha, j].max(axis=0, keepdims=True)
                mx[ha] = bm if j == 0 else jnp.maximum(mx[ha], bm)
            if 0 <= hb < N_HEADS:
                pt[hb, j] = _bf16(jnp.exp2(st.pop((hb, j)) - mx[hb]))
            if 0 <= hc < N_HEADS:
                part = _dot(vt_ref[hc * V_SLOT:(hc + 1) * V_SLOT, blocks[j]], pt.pop((hc, j)))
                acc[hc] = part if j == 0 else acc[hc] + part
        if 0 <= hc < N_HEADS:
            a = acc.pop(hc)
            outs.append(a[:V_HEAD_DIM] / a[V_HEAD_DIM:V_HEAD_DIM + 1])
    o_ref[...] = _bf16(jnp.concatenate(outs, axis=0).T)


def _attention(qt, k, vt, batch, seq):
    t = k.shape[0]
    tq = min(TQ_ATTN, seq)
    per_seq = seq // tq
    return pl.pallas_call(
        _attn_kernel,
        grid=(batch, per_seq),
        in_specs=[
            pl.BlockSpec((None, N_HEADS * HEAD_PAD, tq), lambda b, i: (b, 0, i)),
            pl.BlockSpec((seq, N_HEADS * HEAD_PAD), lambda b, i: (b, 0)),
            pl.BlockSpec((None, N_HEADS * V_SLOT, seq), lambda b, i: (b, 0, 0)),
        ],
        out_specs=pl.BlockSpec((tq, N_HEADS * V_HEAD_DIM), lambda b, i: (b * per_seq + i, 0)),
        out_shape=jax.ShapeDtypeStruct((t, N_HEADS * V_HEAD_DIM), jnp.bfloat16),
        compiler_params=pltpu.CompilerParams(
            dimension_semantics=("arbitrary", "arbitrary"), vmem_limit_bytes=VMEM_LIMIT_BYTES),
        name="attention",
    )(qt, k, vt)


def _merge_ffn_kernel(x_ref, attn_ref, rec_ref, gates_ref, woa_ref, wol_ref, wout_ref,
                      g2_ref, wg_ref, wu_ref, wd_ref, gf_ref, o_ref):
    d = x_ref.shape[1]
    half = x_ref.shape[0] // 2
    rows = [pl.ds(r * half, half) for r in range(2)]
    a = [_dot(attn_ref[rw, :], woa_ref[...]) for rw in rows]
    r = [_dot(rec_ref[rw, :], wol_ref[...]) for rw in rows]
    merged = []
    for i, rw in enumerate(rows):
        gates = gates_ref[rw, :].astype(jnp.float32)
        merged.append(_bf16(gates[:, :d] * a[i] + gates[:, d:] * r[i]))
    x1 = [x_ref[rw, :] + _dot(merged[i], wout_ref[...]) for i, rw in enumerate(rows)]
    h2 = [_bf16(_rms(v, g2_ref[...])) for v in x1]
    gate = [_dot(v, wg_ref[...]) for v in h2]
    up = [_dot(v, wu_ref[...]) for v in h2]
    act = [_bf16(jax.nn.silu(gate[i]) * up[i]) for i in range(2)]
    for i, rw in enumerate(rows):
        o_ref[rw, :] = _rms(x1[i] + _dot(act[i], wd_ref[...]), gf_ref[...])


def _merge_ffn(x2, attn, rec, gates, woa, wol, wout, g2, wg, wu, wd, gf):
    t, d = x2.shape
    tm = min(TM_OUT, t)
    row = lambda i: (i, 0)
    return pl.pallas_call(
        _merge_ffn_kernel,
        grid=(t // tm,),
        in_specs=[
            pl.BlockSpec((tm, d), row), pl.BlockSpec((tm, attn.shape[1]), row),
            pl.BlockSpec((tm, rec.shape[1]), row), pl.BlockSpec((tm, 2 * d), row),
            _const_spec(woa.shape), _const_spec(wol.shape), _const_spec(wout.shape),
            _const_spec(g2.shape), _const_spec(wg.shape), _const_spec(wu.shape),
            _const_spec(wd.shape), _const_spec(gf.shape),
        ],
        out_specs=pl.BlockSpec((tm, d), row),
        out_shape=jax.ShapeDtypeStruct((t, d), jnp.float32),
        compiler_params=pltpu.CompilerParams(
            dimension_semantics=("arbitrary",), vmem_limit_bytes=VMEM_LIMIT_BYTES),
        name="merge_ffn",
    )(x2, attn, rec, gates, woa, wol, wout, g2, wg, wu, wd, gf)


def _rope_slot_tables(seq):
    pos = jnp.arange(seq, dtype=jnp.float32)
    inv_freq = 1.0 / (ROPE_THETA ** (jnp.arange(0, QK_ROPE_DIM, 2, dtype=jnp.float32) / QK_ROPE_DIM))
    ang = pos[:, None] * inv_freq[None, :]
    cos, sin = jnp.cos(ang), jnp.sin(ang)
    half = QK_ROPE_DIM // 2
    ones = jnp.ones((seq, QK_NOPE_DIM), jnp.float32)
    z_half = jnp.zeros((seq, half), jnp.float32)
    z_nope = jnp.zeros((seq, QK_NOPE_DIM), jnp.float32)
    z_pad = jnp.zeros((seq, HEAD_PAD - QK_DIM), jnp.float32)
    rc = jnp.concatenate([ones, cos, cos, z_pad], axis=1)
    rlo = jnp.concatenate([z_nope, -sin, z_half, z_pad], axis=1)
    rhi = jnp.concatenate([z_nope, z_half, sin, z_pad], axis=1)
    return jnp.concatenate([rc, rlo, rhi], axis=1)


def _head_slots(w, width):
    k = w.shape[0]
    w = w.reshape(k, N_HEADS, width)
    w = jnp.pad(w, ((0, 0), (0, 0), (0, HEAD_PAD - width)))
    return w.reshape(k, N_HEADS * HEAD_PAD)


def kernel(x, norm1_g, w_in, q_a_norm_g, w_q_b, kv_a_norm_g, w_kv_b, w_o_attn,
           conv_w, conv_b, w_rgate, b_rgate, w_igate, b_igate, lru_lambda, w_o_lru,
           w_out, norm2_g, w_ffn_gate, w_ffn_up, w_ffn_down, final_g):
    batch, seq, d = x.shape
    depth = w_in.shape[0]
    bf = jnp.bfloat16
    c_qa = Q_LORA_RANK
    c_kva = c_qa + KV_LORA_RANK
    c_kr = c_kva + QK_ROPE_DIM
    rope = _rope_slot_tables(seq)
    assert depth == 1, "the final rmsnorm is fused into the single layer's last kernel"
    l = 0
    x2 = x.reshape(batch * seq, d)

    wi = w_in[l]
    w_kr = jnp.pad(wi[:, c_kva:c_kr], ((0, 0), (QK_NOPE_DIM, HEAD_PAD - QK_DIM)))
    w_in_r = jnp.concatenate([wi[:, :c_kva], w_kr, wi[:, c_kr:]], axis=1).astype(bf)
    wqb_r = _head_slots(w_q_b[l], QK_DIM).astype(bf)
    wkv = w_kv_b[l].reshape(KV_LORA_RANK, N_HEADS, QK_NOPE_DIM + V_HEAD_DIM)
    wk_r = _head_slots(wkv[:, :, :QK_NOPE_DIM].reshape(KV_LORA_RANK, -1), QK_NOPE_DIM)
    wv = wkv[:, :, QK_NOPE_DIM:].reshape(KV_LORA_RANK, -1)
    wkvb_r = jnp.concatenate([wk_r, wv], axis=1).astype(bf)

    qt, k, vt, lx, gy, gates = _in_proj(
        x2, norm1_g[l][None], w_in_r, q_a_norm_g[l][None], wqb_r, kv_a_norm_g[l][None],
        wkvb_r, rope, batch, seq)

    wg = jnp.concatenate([w_rgate[l, 0], w_rgate[l, 1], w_igate[l, 0], w_igate[l, 1]], axis=2).astype(bf)
    bg = 0.5 * jnp.concatenate([b_rgate[l, 0], b_rgate[l, 1], b_igate[l, 0], b_igate[l, 1]], axis=1)[:, None, :]
    bg_hi = bg.astype(bf)
    bg_lo = (bg - bg_hi.astype(jnp.float32)).astype(bf)
    wg = jnp.concatenate([wg, bg_hi, bg_lo, jnp.zeros((LRU_BLOCKS, LRU_BLOCK_DIM - 2, wg.shape[2]), bf)], axis=1)
    rec = _rglru(lx, gy, conv_w[l], conv_b[l][None], wg, lru_lambda[l], batch, seq)

    attn = _attention(qt, k, vt, batch, seq)

    out = _merge_ffn(x2, attn, rec, gates, w_o_attn[l].astype(bf), w_o_lru[l].astype(bf),
                     w_out[l].astype(bf), norm2_g[l][None], w_ffn_gate[l].astype(bf),
                     w_ffn_up[l].astype(bf), w_ffn_down[l].astype(bf), final_g[None])
    return out.reshape(batch, seq, d)
```

```python
import functools

import jax
import jax.numpy as jnp
from jax import lax
from jax.experimental import pallas as pl
from jax.experimental.pallas import tpu as pltpu

N_HEADS = 8
QK_NOPE_DIM = 64
QK_ROPE_DIM = 32
QK_DIM = QK_NOPE_DIM + QK_ROPE_DIM
V_HEAD_DIM = 64
Q_LORA_RANK = 256
KV_LORA_RANK = 128
ROPE_THETA = 10000.0
LRU_WIDTH = 512
LRU_BLOCKS = 4
LRU_BLOCK_DIM = 128
CONV_WIDTH = 4
CONV_LEFT = 2
LRU_C = 8.0
NORM_EPS = 1e-6
LOG2_E = 1.4426950408889634

LANES = 128
SUBLANES = 8
HEAD_PAD = LANES
V_SLOT = 80
VMEM_LIMIT_BYTES = 56 * 1024 * 1024

TM_IN = 512
TQ_ATTN = 512
TK_ATTN = 256
TM_OUT = 512
N_SEG = 2 * SUBLANES
SEG_PAD = 4
SWEEP_UNROLL = 8
SEG_OFF = SUBLANES
SCAN_CHUNK = 64


def _rms(x, g):
    return x * lax.rsqrt(jnp.mean(x * x, axis=-1, keepdims=True) + NORM_EPS) * g


def _bf16(x):
    return x.astype(jnp.bfloat16)


def _dot(a, b):
    return jnp.dot(a, b, preferred_element_type=jnp.float32)


def _const_spec(shape):
    nd = len(shape)
    return pl.BlockSpec(shape, lambda *_: (0,) * nd, pipeline_mode=pl.Buffered(1))


def _rope_slot(t, c, s_lo, s_hi):
    return t * c + pltpu.roll(t, 16, 1) * s_hi + pltpu.roll(t, LANES - 16, 1) * s_lo


def _in_proj_kernel(x_ref, g1_ref, w_in_ref, gq_ref, wqb_ref, gkv_ref, wkvb_ref,
                    rope_ref,
                    qt_ref, k_ref, vt_ref, lx_ref, gy_ref, gates_ref, qn_ref, kvn_ref, kr_ref):
    c0 = Q_LORA_RANK
    c1 = c0 + KV_LORA_RANK
    c2 = c1 + HEAD_PAD
    c3 = c2 + LRU_WIDTH
    c4 = c3 + LRU_WIDTH

    @pl.when(pl.program_id(0) == 0)
    def _():
        for ref in (qn_ref, kvn_ref, kr_ref):
            ref[...] = jnp.zeros(ref.shape, ref.dtype)

    rc = rope_ref[:, :LANES]
    rlo = rope_ref[:, LANES:2 * LANES]
    rhi = rope_ref[:, 2 * LANES:]
    scale = QK_DIM ** -0.5 * LOG2_E
    q = _dot(qn_ref[...], wqb_ref[...])
    kv = _dot(kvn_ref[...], wkvb_ref[...])
    k_rope = _rope_slot(kr_ref[...], rc, rlo, rhi)
    vt = _bf16(kv[:, N_HEADS * HEAD_PAD:].T)
    pad_rows = V_SLOT - V_HEAD_DIM
    ones_row = (lax.broadcasted_iota(jnp.int32, (pad_rows, vt.shape[1]), 0) == 0).astype(vt.dtype)
    for hd in range(N_HEADS):
        sl = slice(hd * HEAD_PAD, (hd + 1) * HEAD_PAD)
        qt_ref[sl, :] = _bf16((_rope_slot(q[:, sl], rc, rlo, rhi) * scale).T)
        k_ref[:, sl] = _bf16(kv[:, sl] + k_rope)
        vt_ref[pl.ds(hd * V_SLOT, V_HEAD_DIM), :] = vt[hd * V_HEAD_DIM:(hd + 1) * V_HEAD_DIM]
        vt_ref[pl.ds(hd * V_SLOT + V_HEAD_DIM, pad_rows), :] = ones_row

    half = x_ref.shape[0] // 2
    for r in range(2):
        rows = pl.ds(r * half, half)
        h = _bf16(_rms(x_ref[rows, :], g1_ref[...]))
        proj = _dot(h, w_in_ref[...])
        qn_ref[rows, :] = _bf16(_rms(proj[:, :c0], gq_ref[...]))
        kvn_ref[rows, :] = _bf16(_rms(proj[:, c0:c1], gkv_ref[...]))
        kr_ref[rows, :] = proj[:, c1:c2]
        lx_ref[rows, :] = proj[:, c2:c3]
        gy_ref[rows, :] = _bf16(jax.nn.gelu(proj[:, c3:c4]))
        gates_ref[rows, :] = _bf16(jax.nn.sigmoid(proj[:, c4:]))


def _in_proj(x2, g1, w_in_r, gq, wqb_r, gkv, wkvb_r, rope, batch, seq):
    t = x2.shape[0]
    d = x2.shape[1]
    tm = min(TM_IN, seq)
    per_seq = seq // tm
    n = t // tm
    cur = lambda i: jnp.minimum(i, n - 1)
    prev = lambda i: jnp.maximum(i - 1, 0)
    row = lambda i: (cur(i), 0)
    prow = lambda i: (prev(i), 0)
    ppos = lambda i: (prev(i) % per_seq, 0)
    pcol = lambda i: (prev(i) // per_seq, 0, prev(i) % per_seq)
    bf = jnp.bfloat16
    return pl.pallas_call(
        _in_proj_kernel,
        grid=(n + 1,),
        in_specs=[
            pl.BlockSpec((tm, d), row),
            _const_spec(g1.shape), _const_spec(w_in_r.shape), _const_spec(gq.shape),
            _const_spec(wqb_r.shape), _const_spec(gkv.shape), _const_spec(wkvb_r.shape),
            pl.BlockSpec((tm, rope.shape[1]), ppos),
        ],
        out_specs=[
            pl.BlockSpec((None, N_HEADS * HEAD_PAD, tm), pcol),
            pl.BlockSpec((tm, N_HEADS * HEAD_PAD), prow),
            pl.BlockSpec((None, N_HEADS * V_SLOT, tm), pcol),
            pl.BlockSpec((tm, LRU_WIDTH), row),
            pl.BlockSpec((tm, LRU_WIDTH), row),
            pl.BlockSpec((tm, 2 * d), row),
        ],
        out_shape=[
            jax.ShapeDtypeStruct((batch, N_HEADS * HEAD_PAD, seq), bf),
            jax.ShapeDtypeStruct((t, N_HEADS * HEAD_PAD), bf),
            jax.ShapeDtypeStruct((batch, N_HEADS * V_SLOT, seq), bf),
            jax.ShapeDtypeStruct((t, LRU_WIDTH), jnp.float32),
            jax.ShapeDtypeStruct((t, LRU_WIDTH), bf),
            jax.ShapeDtypeStruct((t, 2 * d), bf),
        ],
        scratch_shapes=[pltpu.VMEM((tm, Q_LORA_RANK), bf), pltpu.VMEM((tm, KV_LORA_RANK), bf),
                        pltpu.VMEM((tm, HEAD_PAD), jnp.float32)],
        compiler_params=pltpu.CompilerParams(
            dimension_semantics=("arbitrary",), vmem_limit_bytes=VMEM_LIMIT_BYTES),
        name="in_proj",
    )(x2, g1, w_in_r, gq, wqb_r, gkv, wkvb_r, rope)


def _shift_rows(x, down):
    rows = lax.broadcasted_iota(jnp.int32, x.shape, 0)
    if down:
        return jnp.where(rows == 0, 0.0, pltpu.roll(x, 1, 0))
    return jnp.where(rows == x.shape[0] - 1, 0.0, pltpu.roll(x, x.shape[0] - 1, 0))


def _rglru_kernel(lx_ref, gy_ref, cw_ref, cb_ref, wg_ref, lam_ref, o_ref,
                  x_ref, a_ref, u_ref, h_ref, *, seq):
    seg = seq // N_SEG
    pitch = seg + SEG_PAD
    nb = LRU_BLOCKS
    jc = min(SCAN_CHUNK, seg)
    f32 = jnp.float32

    zeros = jnp.zeros((SUBLANES, LANES), f32)
    for n in range(nb):
        ln = slice(n * LANES, (n + 1) * LANES)
        x_ref[n, pl.ds(0, SUBLANES), :] = zeros
        x_ref[n, pl.ds(SEG_OFF + N_SEG * pitch - SEG_PAD, SUBLANES), :] = zeros
        for s in range(N_SEG):
            x_ref[n, pl.ds(SEG_OFF + s * pitch, seg), :] = lx_ref[pl.ds(s * seg, seg), ln]
        for s in range(1, N_SEG):
            x_ref[n, pl.ds(SEG_OFF + s * pitch - CONV_LEFT, CONV_LEFT), :] = (
                lx_ref[pl.ds(s * seg - CONV_LEFT, CONV_LEFT), ln])
            x_ref[n, pl.ds(SEG_OFF + (s - 1) * pitch + seg, 1), :] = lx_ref[pl.ds(s * seg, 1), ln]

    lam = lam_ref[...]
    z = -lam
    softplus = jnp.maximum(z, 0.0) + jnp.log1p(jnp.exp(-jnp.abs(z)))
    half_decay2 = (-0.5 * LRU_C * LOG2_E) * softplus
    cw = 0.5 * cw_ref[...]
    cb = 0.5 * cb_ref[...]

    def gates(c, _):
        j0 = c * jc
        for n in range(nb):
            ln = slice(n * LANES, (n + 1) * LANES)
            xv = [x_ref.at[n][pl.ds(SEG_OFF - CONV_LEFT + j0 + r, N_SEG, stride=pitch), :]
                  for r in range(jc + CONV_WIDTH - 1)]
            hx = cb[:, ln] + sum(
                jnp.concatenate(xv[o:o + jc], axis=0) * cw[o:o + 1, ln] for o in range(CONV_WIDTH))
            lhs = _bf16(hx)
            lhs = jnp.concatenate([lhs, jnp.ones_like(lhs)], axis=1)
            t = jnp.tanh(_dot(lhs, wg_ref[n]))
            rows = pl.ds(pl.multiple_of(j0 * N_SEG, jc * N_SEG), jc * N_SEG)
            for d in range(2):
                hd = half_decay2[d:d + 1, ln]
                a = jnp.exp2(t[:, d * LANES:(d + 1) * LANES] * hd + hd)
                m2 = 1.0 - a * a
                mult = jnp.where(m2 > 0.0, m2 * lax.rsqrt(m2), 0.0)
                a_ref[d * nb + n, rows, :] = a
                u_ref[d * nb + n, rows, :] = (t[:, (2 + d) * LANES:(3 + d) * LANES] + 1.0) * (hx * mult)
        return 0

    lax.fori_loop(0, seg // jc, gates, 0)

    def row(j):
        return (pl.ds(pl.multiple_of(j * N_SEG, N_SEG), N_SEG), slice(None))

    def sweep1(j, carry):
        out = []
        for k in range(2 * nb):
            h, p = carry[k]
            jj = j if k < nb else seg - 1 - j
            a = a_ref.at[k][row(jj)]
            out.append((a * h + u_ref.at[k][row(jj)], a * p))
        return tuple(out)

    zero = jnp.zeros((N_SEG, LANES), f32)
    one = jnp.ones((N_SEG, LANES), f32)
    ends = lax.fori_loop(0, seg, sweep1, tuple((zero, one) for _ in range(2 * nb)),
                         unroll=SWEEP_UNROLL)

    starts = []
    for k in range(2 * nb):
        h_end, p_end = ends[k]
        st = zero
        for _ in range(N_SEG - 1):
            st = _shift_rows(h_end + p_end * st, down=(k < nb))
        starts.append(st)

    def sweep2(j, carry):
        out = []
        for k in range(2 * nb):
            jj = j if k < nb else seg - 1 - j
            h = a_ref.at[k][row(jj)] * carry[k] + u_ref.at[k][row(jj)]
            h_ref.at[k][row(jj)] = h
            out.append(h)
        return tuple(out)

    lax.fori_loop(0, seg, sweep2, tuple(starts), unroll=SWEEP_UNROLL)

    def unscan(j, _):
        for n in range(nb):
            hsum = h_ref.at[n][row(j)] + h_ref.at[nb + n][row(j)]
            x_ref.at[n][pl.ds(SEG_OFF + j, N_SEG, stride=pitch), :] = hsum
        return 0

    lax.fori_loop(0, seg, unscan, 0, unroll=SWEEP_UNROLL)

    for s in range(N_SEG):
        hsum = jnp.concatenate(
            [x_ref[n, pl.ds(SEG_OFF + s * pitch, seg), :] for n in range(nb)], axis=1)
        o_ref[pl.ds(s * seg, seg), :] = _bf16(gy_ref[pl.ds(s * seg, seg), :].astype(f32) * hsum)


def _rglru(lx, gy, cw, cb, wg, lam, batch, seq):
    t = lx.shape[0]
    pitch = seq // N_SEG + SEG_PAD
    row = lambda b: (b, 0)
    scan_buf = pltpu.VMEM((2 * LRU_BLOCKS, seq, LANES), jnp.float32)
    return pl.pallas_call(
        functools.partial(_rglru_kernel, seq=seq),
        grid=(batch,),
        in_specs=[
            pl.BlockSpec((seq, LRU_WIDTH), row), pl.BlockSpec((seq, LRU_WIDTH), row),
            _const_spec(cw.shape), _const_spec(cb.shape), _const_spec(wg.shape),
            _const_spec(lam.shape),
        ],
        out_specs=pl.BlockSpec((seq, LRU_WIDTH), row),
        out_shape=jax.ShapeDtypeStruct((t, LRU_WIDTH), jnp.bfloat16),
        scratch_shapes=[
            pltpu.VMEM((LRU_BLOCKS, SEG_OFF + N_SEG * pitch + SUBLANES, LANES), jnp.float32),
            scan_buf, scan_buf, scan_buf,
        ],
        compiler_params=pltpu.CompilerParams(
            dimension_semantics=("arbitrary",), vmem_limit_bytes=VMEM_LIMIT_BYTES),
        name="rglru",
    )(lx, gy, cw, cb, wg, lam)


def _attn_kernel(qt_ref, k_ref, vt_ref, o_ref):
    seq = k_ref.shape[0]
    nblk = seq // TK_ATTN
    heads = [slice(hd * HEAD_PAD, (hd + 1) * HEAD_PAD) for hd in range(N_HEADS)]
    blocks = [slice(j * TK_ATTN, (j + 1) * TK_ATTN) for j in range(nblk)]
    st = {}
    pt = {}
    mx = {}
    acc = {}
    outs = []
    for it in range(N_HEADS + 2):
        ha, hb, hc = it, it - 1, it - 2
        for j in range(nblk):
            if ha < N_HEADS:
                st[ha, j] = _dot(k_ref[blocks[j], heads[ha]], qt_ref[heads[ha], :])
                bm = st[ha, j].max(axis=0, keepdims=True)
                mx[ha] = bm if j == 0 else jnp.maximum(mx[ha], bm)
            if 0 <= hb < N_HEADS:
                pt[hb, j] = _bf16(jnp.exp2(st.pop((hb, j)) - mx[hb]))
            if 0 <= hc < N_HEADS:
                part = _dot(vt_ref[hc * V_SLOT:(hc + 1) * V_SLOT, blocks[j]], pt.pop((hc, j)))
                acc[hc] = part if j == 0 else acc[hc] + part
        if 0 <= hc < N_HEADS:
            a = acc.pop(hc)
            outs.append(a[:V_HEAD_DIM] / a[V_HEAD_DIM:V_HEAD_DIM + 1])
    o_ref[...] = _bf16(jnp.concatenate(outs, axis=0).T)


def _attention(qt, k, vt, batch, seq):
    t = k.shape[0]
    tq = min(TQ_ATTN, seq)
    per_seq = seq // tq
    return pl.pallas_call(
        _attn_kernel,
        grid=(batch, per_seq),
        in_specs=[
            pl.BlockSpec((None, N_HEADS * HEAD_PAD, tq), lambda b, i: (b, 0, i)),
            pl.BlockSpec((seq, N_HEADS * HEAD_PAD), lambda b, i: (b, 0)),
            pl.BlockSpec((None, N_HEADS * V_SLOT, seq), lambda b, i: (b, 0, 0)),
        ],
        out_specs=pl.BlockSpec((tq, N_HEADS * V_HEAD_DIM), lambda b, i: (b * per_seq + i, 0)),
        out_shape=jax.ShapeDtypeStruct((t, N_HEADS * V_HEAD_DIM), jnp.bfloat16),
        compiler_params=pltpu.CompilerParams(
            dimension_semantics=("arbitrary", "arbitrary"), vmem_limit_bytes=VMEM_LIMIT_BYTES),
        name="attention",
    )(qt, k, vt)


def _merge_ffn_kernel(x_ref, attn_ref, rec_ref, gates_ref, woa_ref, wol_ref, wout_ref,
                      g2_ref, wg_ref, wu_ref, wd_ref, gf_ref, o_ref):
    d = x_ref.shape[1]
    half = x_ref.shape[0] // 2
    rows = [pl.ds(r * half, half) for r in range(2)]
    a = [_dot(attn_ref[rw, :], woa_ref[...]) for rw in rows]
    r = [_dot(rec_ref[rw, :], wol_ref[...]) for rw in rows]
    merged = []
    for i, rw in enumerate(rows):
        gates = gates_ref[rw, :].astype(jnp.float32)
        merged.append(_bf16(gates[:, :d] * a[i] + gates[:, d:] * r[i]))
    x1 = [x_ref[rw, :] + _dot(merged[i], wout_ref[...]) for i, rw in enumerate(rows)]
    h2 = [_bf16(_rms(v, g2_ref[...])) for v in x1]
    gate = [_dot(v, wg_ref[...]) for v in h2]
    up = [_dot(v, wu_ref[...]) for v in h2]
    act = [_bf16(jax.nn.silu(gate[i]) * up[i]) for i in range(2)]
    for i, rw in enumerate(rows):
        o_ref[rw, :] = _rms(x1[i] + _dot(act[i], wd_ref[...]), gf_ref[...])


def _merge_ffn(x2, attn, rec, gates, woa, wol, wout, g2, wg, wu, wd, gf):
    t, d = x2.shape
    tm = min(TM_OUT, t)
    row = lambda i: (i, 0)
    return pl.pallas_call(
        _merge_ffn_kernel,
        grid=(t // tm,),
        in_specs=[
            pl.BlockSpec((tm, d), row), pl.BlockSpec((tm, attn.shape[1]), row),
            pl.BlockSpec((tm, rec.shape[1]), row), pl.BlockSpec((tm, 2 * d), row),
            _const_spec(woa.shape), _const_spec(wol.shape), _const_spec(wout.shape),
            _const_spec(g2.shape), _const_spec(wg.shape), _const_spec(wu.shape),
            _const_spec(wd.shape), _const_spec(gf.shape),
        ],
        out_specs=pl.BlockSpec((tm, d), row),
        out_shape=jax.ShapeDtypeStruct((t, d), jnp.float32),
        compiler_params=pltpu.CompilerParams(
            dimension_semantics=("arbitrary",), vmem_limit_bytes=VMEM_LIMIT_BYTES),
        name="merge_ffn",
    )(x2, attn, rec, gates, woa, wol, wout, g2, wg, wu, wd, gf)


def _rope_slot_tables(seq):
    pos = jnp.arange(seq, dtype=jnp.float32)
    inv_freq = 1.0 / (ROPE_THETA ** (jnp.arange(0, QK_ROPE_DIM, 2, dtype=jnp.float32) / QK_ROPE_DIM))
    ang = pos[:, None] * inv_freq[None, :]
    cos, sin = jnp.cos(ang), jnp.sin(ang)
    half = QK_ROPE_DIM // 2
    ones = jnp.ones((seq, QK_NOPE_DIM), jnp.float32)
    z_half = jnp.zeros((seq, half), jnp.float32)
    z_nope = jnp.zeros((seq, QK_NOPE_DIM), jnp.float32)
    z_pad = jnp.zeros((seq, HEAD_PAD - QK_DIM), jnp.float32)
    rc = jnp.concatenate([ones, cos, cos, z_pad], axis=1)
    rlo = jnp.concatenate([z_nope, -sin, z_half, z_pad], axis=1)
    rhi = jnp.concatenate([z_nope, z_half, sin, z_pad], axis=1)
    return jnp.concatenate([rc, rlo, rhi], axis=1)


def _head_slots(w, width):
    k = w.shape[0]
    w = w.reshape(k, N_HEADS, width)
    w = jnp.pad(w, ((0, 0), (0, 0), (0, HEAD_PAD - width)))
    return w.reshape(k, N_HEADS * HEAD_PAD)


def kernel(x, norm1_g, w_in, q_a_norm_g, w_q_b, kv_a_norm_g, w_kv_b, w_o_attn,
           conv_w, conv_b, w_rgate, b_rgate, w_igate, b_igate, lru_lambda, w_o_lru,
           w_out, norm2_g, w_ffn_gate, w_ffn_up, w_ffn_down, final_g):
    batch, seq, d = x.shape
    depth = w_in.shape[0]
    bf = jnp.bfloat16
    c_qa = Q_LORA_RANK
    c_kva = c_qa + KV_LORA_RANK
    c_kr = c_kva + QK_ROPE_DIM
    rope = _rope_slot_tables(seq)
    assert depth == 1, "the final rmsnorm is fused into the single layer's last kernel"
    l = 0
    x2 = x.reshape(batch * seq, d)

    wi = w_in[l]
    w_kr = jnp.pad(wi[:, c_kva:c_kr], ((0, 0), (QK_NOPE_DIM, HEAD_PAD - QK_DIM)))
    w_in_r = jnp.concatenate([wi[:, :c_kva], w_kr, wi[:, c_kr:]], axis=1).astype(bf)
    wqb_r = _head_slots(w_q_b[l], QK_DIM).astype(bf)
    wkv = w_kv_b[l].reshape(KV_LORA_RANK, N_HEADS, QK_NOPE_DIM + V_HEAD_DIM)
    wk_r = _head_slots(wkv[:, :, :QK_NOPE_DIM].reshape(KV_LORA_RANK, -1), QK_NOPE_DIM)
    wv = wkv[:, :, QK_NOPE_DIM:].reshape(KV_LORA_RANK, -1)
    wkvb_r = jnp.concatenate([wk_r, wv], axis=1).astype(bf)

    qt, k, vt, lx, gy, gates = _in_proj(
        x2, norm1_g[l][None], w_in_r, q_a_norm_g[l][None], wqb_r, kv_a_norm_g[l][None],
        wkvb_r, rope, batch, seq)

    wg = jnp.concatenate([w_rgate[l, 0], w_rgate[l, 1], w_igate[l, 0], w_igate[l, 1]], axis=2).astype(bf)
    bg = 0.5 * jnp.concatenate([b_rgate[l, 0], b_rgate[l, 1], b_igate[l, 0], b_igate[l, 1]], axis=1)[:, None, :]
    bg_hi = bg.astype(bf)
    bg_lo = (bg - bg_hi.astype(jnp.float32)).astype(bf)
    wg = jnp.concatenate([wg, bg_hi, bg_lo, jnp.zeros((LRU_BLOCKS, LRU_BLOCK_DIM - 2, wg.shape[2]), bf)], axis=1)
    rec = _rglru(lx, gy, conv_w[l], conv_b[l][None], wg, lru_lambda[l], batch, seq)

    attn = _attention(qt, k, vt, batch, seq)

    out = _merge_ffn(x2, attn, rec, gates, w_o_attn[l].astype(bf), w_o_lru[l].astype(bf),
                     w_out[l].astype(bf), norm2_g[l][None], w_ffn_gate[l].astype(bf),
                     w_ffn_up[l].astype(bf), w_ffn_down[l].astype(bf), final_g[None])
    return out.reshape(batch, seq, d)
```

```python
import functools

import jax
import jax.numpy as jnp
from jax import lax
from jax.experimental import pallas as pl
from jax.experimental.pallas import tpu as pltpu

N_HEADS = 8
QK_NOPE_DIM = 64
QK_ROPE_DIM = 32
QK_DIM = QK_NOPE_DIM + QK_ROPE_DIM
V_HEAD_DIM = 64
Q_LORA_RANK = 256
KV_LORA_RANK = 128
ROPE_THETA = 10000.0
LRU_WIDTH = 512
LRU_BLOCKS = 4
LRU_BLOCK_DIM = 128
CONV_WIDTH = 4
CONV_LEFT = 2
LRU_C = 8.0
NORM_EPS = 1e-6
LOG2_E = 1.4426950408889634

LANES = 128
SUBLANES = 8
HEAD_PAD = LANES
V_SLOT = 80
VMEM_LIMIT_BYTES = 56 * 1024 * 1024

TM_IN = 512
TQ_ATTN = 512
TK_ATTN = 256
TM_OUT = 512
FF_CHUNK = 1536
N_SEG = 2 * SUBLANES
SEG_PAD = 4
SWEEP_UNROLL = 8
SEG_OFF = SUBLANES
SCAN_CHUNK = 64


def _rms(x, g):
    return x * lax.rsqrt(jnp.mean(x * x, axis=-1, keepdims=True) + NORM_EPS) * g


def _bf16(x):
    return x.astype(jnp.bfloat16)


def _dot(a, b):
    return jnp.dot(a, b, preferred_element_type=jnp.float32)


def _const_spec(shape):
    nd = len(shape)
    return pl.BlockSpec(shape, lambda *_: (0,) * nd, pipeline_mode=pl.Buffered(1))


def _rope_slot(t, c, s_lo, s_hi):
    return t * c + pltpu.roll(t, 16, 1) * s_hi + pltpu.roll(t, LANES - 16, 1) * s_lo


def _in_proj_kernel(x_ref, g1_ref, w_in_ref, gq_ref, wqb_ref, gkv_ref, wkvb_ref,
                    rope_ref,
                    qt_ref, k_ref, vt_ref, lx_ref, gy_ref, gates_ref, qn_ref, kvn_ref, kr_ref):
    c0 = Q_LORA_RANK
    c1 = c0 + KV_LORA_RANK
    c2 = c1 + HEAD_PAD
    c3 = c2 + LRU_WIDTH
    c4 = c3 + LRU_WIDTH

    @pl.when(pl.program_id(0) == 0)
    def _():
        for ref in (qn_ref, kvn_ref, kr_ref):
            ref[...] = jnp.zeros(ref.shape, ref.dtype)

    rc = rope_ref[:, :LANES]
    rlo = rope_ref[:, LANES:2 * LANES]
    rhi = rope_ref[:, 2 * LANES:]
    scale = QK_DIM ** -0.5 * LOG2_E
    q = _dot(qn_ref[...], wqb_ref[...])
    kv = _dot(kvn_ref[...], wkvb_ref[...])
    k_rope = _rope_slot(kr_ref[...], rc, rlo, rhi)
    vt = _bf16(kv[:, N_HEADS * HEAD_PAD:].T)
    pad_rows = V_SLOT - V_HEAD_DIM
    ones_row = (lax.broadcasted_iota(jnp.int32, (pad_rows, vt.shape[1]), 0) == 0).astype(vt.dtype)
    for hd in range(N_HEADS):
        sl = slice(hd * HEAD_PAD, (hd + 1) * HEAD_PAD)
        qt_ref[sl, :] = _bf16((_rope_slot(q[:, sl], rc, rlo, rhi) * scale).T)
        k_ref[:, sl] = _bf16(kv[:, sl] + k_rope)
        vt_ref[pl.ds(hd * V_SLOT, V_HEAD_DIM), :] = vt[hd * V_HEAD_DIM:(hd + 1) * V_HEAD_DIM]
        vt_ref[pl.ds(hd * V_SLOT + V_HEAD_DIM, pad_rows), :] = ones_row

    half = x_ref.shape[0] // 2
    for r in range(2):
        rows = pl.ds(r * half, half)
        h = _bf16(_rms(x_ref[rows, :], g1_ref[...]))
        proj = _dot(h, w_in_ref[...])
        qn_ref[rows, :] = _bf16(_rms(proj[:, :c0], gq_ref[...]))
        kvn_ref[rows, :] = _bf16(_rms(proj[:, c0:c1], gkv_ref[...]))
        kr_ref[rows, :] = proj[:, c1:c2]
        lx_ref[rows, :] = proj[:, c2:c3]
        gy_ref[rows, :] = _bf16(jax.nn.gelu(proj[:, c3:c4]))
        gates_ref[rows, :] = _bf16(jax.nn.sigmoid(proj[:, c4:]))


def _in_proj(x2, g1, w_in_r, gq, wqb_r, gkv, wkvb_r, rope, batch, seq):
    t = x2.shape[0]
    d = x2.shape[1]
    tm = min(TM_IN, seq)
    per_seq = seq // tm
    n = t // tm
    cur = lambda i: jnp.minimum(i, n - 1)
    prev = lambda i: jnp.maximum(i - 1, 0)
    row = lambda i: (cur(i), 0)
    prow = lambda i: (prev(i), 0)
    ppos = lambda i: (prev(i) % per_seq, 0)
    pcol = lambda i: (prev(i) // per_seq, 0, prev(i) % per_seq)
    bf = jnp.bfloat16
    return pl.pallas_call(
        _in_proj_kernel,
        grid=(n + 1,),
        in_specs=[
            pl.BlockSpec((tm, d), row),
            _const_spec(g1.shape), _const_spec(w_in_r.shape), _const_spec(gq.shape),
            _const_spec(wqb_r.shape), _const_spec(gkv.shape), _const_spec(wkvb_r.shape),
            pl.BlockSpec((tm, rope.shape[1]), ppos),
        ],
        out_specs=[
            pl.BlockSpec((None, N_HEADS * HEAD_PAD, tm), pcol),
            pl.BlockSpec((tm, N_HEADS * HEAD_PAD), prow),
            pl.BlockSpec((None, N_HEADS * V_SLOT, tm), pcol),
            pl.BlockSpec((tm, LRU_WIDTH), row),
            pl.BlockSpec((tm, LRU_WIDTH), row),
            pl.BlockSpec((tm, 2 * d), row),
        ],
        out_shape=[
            jax.ShapeDtypeStruct((batch, N_HEADS * HEAD_PAD, seq), bf),
            jax.ShapeDtypeStruct((t, N_HEADS * HEAD_PAD), bf),
            jax.ShapeDtypeStruct((batch, N_HEADS * V_SLOT, seq), bf),
            jax.ShapeDtypeStruct((t, LRU_WIDTH), jnp.float32),
            jax.ShapeDtypeStruct((t, LRU_WIDTH), bf),
            jax.ShapeDtypeStruct((t, 2 * d), bf),
        ],
        scratch_shapes=[pltpu.VMEM((tm, Q_LORA_RANK), bf), pltpu.VMEM((tm, KV_LORA_RANK), bf),
                        pltpu.VMEM((tm, HEAD_PAD), jnp.float32)],
        compiler_params=pltpu.CompilerParams(
            dimension_semantics=("arbitrary",), vmem_limit_bytes=VMEM_LIMIT_BYTES),
        name="in_proj",
    )(x2, g1, w_in_r, gq, wqb_r, gkv, wkvb_r, rope)


def _shift_rows(x, down):
    rows = lax.broadcasted_iota(jnp.int32, x.shape, 0)
    if down:
        return jnp.where(rows == 0, 0.0, pltpu.roll(x, 1, 0))
    return jnp.where(rows == x.shape[0] - 1, 0.0, pltpu.roll(x, x.shape[0] - 1, 0))


def _rglru_kernel(lx_ref, gy_ref, cw_ref, cb_ref, wg_ref, lam_ref, o_ref,
                  x_ref, a_ref, u_ref, h_ref, *, seq):
    seg = seq // N_SEG
    pitch = seg + SEG_PAD
    nb = LRU_BLOCKS
    jc = min(SCAN_CHUNK, seg)
    f32 = jnp.float32

    zeros = jnp.zeros((SUBLANES, LANES), f32)
    for n in range(nb):
        ln = slice(n * LANES, (n + 1) * LANES)
        x_ref[n, pl.ds(0, SUBLANES), :] = zeros
        x_ref[n, pl.ds(SEG_OFF + N_SEG * pitch - SEG_PAD, SUBLANES), :] = zeros
        for s in range(N_SEG):
            x_ref[n, pl.ds(SEG_OFF + s * pitch, seg), :] = lx_ref[pl.ds(s * seg, seg), ln]
        for s in range(1, N_SEG):
            x_ref[n, pl.ds(SEG_OFF + s * pitch - CONV_LEFT, CONV_LEFT), :] = (
                lx_ref[pl.ds(s * seg - CONV_LEFT, CONV_LEFT), ln])
            x_ref[n, pl.ds(SEG_OFF + (s - 1) * pitch + seg, 1), :] = lx_ref[pl.ds(s * seg, 1), ln]

    lam = lam_ref[...]
    z = -lam
    softplus = jnp.maximum(z, 0.0) + jnp.log1p(jnp.exp(-jnp.abs(z)))
    half_decay2 = (-0.5 * LRU_C * LOG2_E) * softplus
    cw = 0.5 * cw_ref[...]
    cb = 0.5 * cb_ref[...]

    def gates(c, _):
        j0 = c * jc
        for n in range(nb):
            ln = slice(n * LANES, (n + 1) * LANES)
            xv = [x_ref.at[n][pl.ds(SEG_OFF - CONV_LEFT + j0 + r, N_SEG, stride=pitch), :]
                  for r in range(jc + CONV_WIDTH - 1)]
            hx = cb[:, ln] + sum(
                jnp.concatenate(xv[o:o + jc], axis=0) * cw[o:o + 1, ln] for o in range(CONV_WIDTH))
            lhs = _bf16(hx)
            lhs = jnp.concatenate([lhs, jnp.ones_like(lhs)], axis=1)
            t = jnp.tanh(_dot(lhs, wg_ref[n]))
            rows = pl.ds(pl.multiple_of(j0 * N_SEG, jc * N_SEG), jc * N_SEG)
            for d in range(2):
                hd = half_decay2[d:d + 1, ln]
                a = jnp.exp2(t[:, d * LANES:(d + 1) * LANES] * hd + hd)
                m2 = 1.0 - a * a
                mult = jnp.where(m2 > 0.0, m2 * lax.rsqrt(m2), 0.0)
                a_ref[d * nb + n, rows, :] = a
                u_ref[d * nb + n, rows, :] = (t[:, (2 + d) * LANES:(3 + d) * LANES] + 1.0) * (hx * mult)
        return 0

    lax.fori_loop(0, seg // jc, gates, 0)

    def row(j):
        return (pl.ds(pl.multiple_of(j * N_SEG, N_SEG), N_SEG), slice(None))

    def sweep1(j, carry):
        out = []
        for k in range(2 * nb):
            h, p = carry[k]
            jj = j if k < nb else seg - 1 - j
            a = a_ref.at[k][row(jj)]
            out.append((a * h + u_ref.at[k][row(jj)], a * p))
        return tuple(out)

    zero = jnp.zeros((N_SEG, LANES), f32)
    one = jnp.ones((N_SEG, LANES), f32)
    ends = lax.fori_loop(0, seg, sweep1, tuple((zero, one) for _ in range(2 * nb)),
                         unroll=SWEEP_UNROLL)

    starts = []
    for k in range(2 * nb):
        h_end, p_end = ends[k]
        st = zero
        for _ in range(N_SEG - 1):
            st = _shift_rows(h_end + p_end * st, down=(k < nb))
        starts.append(st)

    def sweep2(j, carry):
        out = []
        for k in range(2 * nb):
            jj = j if k < nb else seg - 1 - j
            h = a_ref.at[k][row(jj)] * carry[k] + u_ref.at[k][row(jj)]
            h_ref.at[k][row(jj)] = h
            out.append(h)
        return tuple(out)

    lax.fori_loop(0, seg, sweep2, tuple(starts), unroll=SWEEP_UNROLL)

    def unscan(j, _):
        for n in range(nb):
            hsum = h_ref.at[n][row(j)] + h_ref.at[nb + n][row(j)]
            x_ref.at[n][pl.ds(SEG_OFF + j, N_SEG, stride=pitch), :] = hsum
        return 0

    lax.fori_loop(0, seg, unscan, 0, unroll=SWEEP_UNROLL)

    for s in range(N_SEG):
        hsum = jnp.concatenate(
            [x_ref[n, pl.ds(SEG_OFF + s * pitch, seg), :] for n in range(nb)], axis=1)
        o_ref[pl.ds(s * seg, seg), :] = _bf16(gy_ref[pl.ds(s * seg, seg), :].astype(f32) * hsum)


def _rglru(lx, gy, cw, cb, wg, lam, batch, seq):
    t = lx.shape[0]
    pitch = seq // N_SEG + SEG_PAD
    row = lambda b: (b, 0)
    scan_buf = pltpu.VMEM((2 * LRU_BLOCKS, seq, LANES), jnp.float32)
    return pl.pallas_call(
        functools.partial(_rglru_kernel, seq=seq),
        grid=(batch,),
        in_specs=[
            pl.BlockSpec((seq, LRU_WIDTH), row), pl.BlockSpec((seq, LRU_WIDTH), row),
            _const_spec(cw.shape), _const_spec(cb.shape), _const_spec(wg.shape),
            _const_spec(lam.shape),
        ],
        out_specs=pl.BlockSpec((seq, LRU_WIDTH), row),
        out_shape=jax.ShapeDtypeStruct((t, LRU_WIDTH), jnp.bfloat16),
        scratch_shapes=[
            pltpu.VMEM((LRU_BLOCKS, SEG_OFF + N_SEG * pitch + SUBLANES, LANES), jnp.float32),
            scan_buf, scan_buf, scan_buf,
        ],
        compiler_params=pltpu.CompilerParams(
            dimension_semantics=("arbitrary",), vmem_limit_bytes=VMEM_LIMIT_BYTES),
        name="rglru",
    )(lx, gy, cw, cb, wg, lam)


def _attn_kernel(qt_ref, k_ref, vt_ref, o_ref):
    seq = k_ref.shape[0]
    nblk = seq // TK_ATTN
    heads = [slice(hd * HEAD_PAD, (hd + 1) * HEAD_PAD) for hd in range(N_HEADS)]
    blocks = [slice(j * TK_ATTN, (j + 1) * TK_ATTN) for j in range(nblk)]
    st = {}
    pt = {}
    mx = {}
    acc = {}
    outs = []
    for it in range(N_HEADS + 2):
        ha, hb, hc = it, it - 1, it - 2
        for j in range(nblk):
            if ha < N_HEADS:
                st[ha, j] = _dot(k_ref[blocks[j], heads[ha]], qt_ref[heads[ha], :])
                bm = st[ha, j].max(axis=0, keepdims=True)
                mx[ha] = bm if j == 0 else jnp.maximum(mx[ha], bm)
            if 0 <= hb < N_HEADS:
                pt[hb, j] = _bf16(jnp.exp2(st.pop((hb, j)) - mx[hb]))
            if 0 <= hc < N_HEADS:
                part = _dot(vt_ref[hc * V_SLOT:(hc + 1) * V_SLOT, blocks[j]], pt.pop((hc, j)))
                acc[hc] = part if j == 0 else acc[hc] + part
        if 0 <= hc < N_HEADS:
            a = acc.pop(hc)
            outs.append(a[:V_HEAD_DIM] / a[V_HEAD_DIM:V_HEAD_DIM + 1])
    o_ref[...] = _bf16(jnp.concatenate(outs, axis=0).T)


def _attention(qt, k, vt, batch, seq):
    t = k.shape[0]
    tq = min(TQ_ATTN, seq)
    per_seq = seq // tq
    return pl.pallas_call(
        _attn_kernel,
        grid=(batch, per_seq),
        in_specs=[
            pl.BlockSpec((None, N_HEADS * HEAD_PAD, tq), lambda b, i: (b, 0, i)),
            pl.BlockSpec((seq, N_HEADS * HEAD_PAD), lambda b, i: (b, 0)),
            pl.BlockSpec((None, N_HEADS * V_SLOT, seq), lambda b, i: (b, 0, 0)),
        ],
        out_specs=pl.BlockSpec((tq, N_HEADS * V_HEAD_DIM), lambda b, i: (b * per_seq + i, 0)),
        out_shape=jax.ShapeDtypeStruct((t, N_HEADS * V_HEAD_DIM), jnp.bfloat16),
        compiler_params=pltpu.CompilerParams(
            dimension_semantics=("arbitrary", "arbitrary"), vmem_limit_bytes=VMEM_LIMIT_BYTES),
        name="attention",
    )(qt, k, vt)


def _merge_ffn_kernel(x_ref, attn_ref, rec_ref, gates_ref, woa_ref, wol_ref, wout_ref,
                      g2_ref, wg_ref, wu_ref, wd_ref, gf_ref, o_ref):
    d = x_ref.shape[1]
    half = x_ref.shape[0] // 2
    rows = [pl.ds(r * half, half) for r in range(2)]
    a = [_dot(attn_ref[rw, :], woa_ref[...]) for rw in rows]
    r = [_dot(rec_ref[rw, :], wol_ref[...]) for rw in rows]
    merged = []
    for i, rw in enumerate(rows):
        gates = gates_ref[rw, :].astype(jnp.float32)
        merged.append(_bf16(gates[:, :d] * a[i] + gates[:, d:] * r[i]))
    x1 = [x_ref[rw, :] + _dot(merged[i], wout_ref[...]) for i, rw in enumerate(rows)]
    h2 = [_bf16(_rms(v, g2_ref[...])) for v in x1]
    d_ff = wg_ref.shape[1]
    chunks = [(c, min(FF_CHUNK, d_ff - c)) for c in range(0, d_ff, FF_CHUNK)]
    down = [None, None]
    for c, w in chunks:
        for i in range(2):
            gate = _dot(h2[i], wg_ref[:, c:c + w])
            up = _dot(h2[i], wu_ref[:, c:c + w])
            part = _dot(_bf16(jax.nn.silu(gate) * up), wd_ref[c:c + w, :])
            down[i] = part if down[i] is None else down[i] + part
    for i, rw in enumerate(rows):
        o_ref[rw, :] = _rms(x1[i] + down[i], gf_ref[...])


def _merge_ffn(x2, attn, rec, gates, woa, wol, wout, g2, wg, wu, wd, gf):
    t, d = x2.shape
    tm = min(TM_OUT, t)
    row = lambda i: (i, 0)
    return pl.pallas_call(
        _merge_ffn_kernel,
        grid=(t // tm,),
        in_specs=[
            pl.BlockSpec((tm, d), row), pl.BlockSpec((tm, attn.shape[1]), row),
            pl.BlockSpec((tm, rec.shape[1]), row), pl.BlockSpec((tm, 2 * d), row),
            _const_spec(woa.shape), _const_spec(wol.shape), _const_spec(wout.shape),
            _const_spec(g2.shape), _const_spec(wg.shape), _const_spec(wu.shape),
            _const_spec(wd.shape), _const_spec(gf.shape),
        ],
        out_specs=pl.BlockSpec((tm, d), row),
        out_shape=jax.ShapeDtypeStruct((t, d), jnp.float32),
        compiler_params=pltpu.CompilerParams(
            dimension_semantics=("arbitrary",), vmem_limit_bytes=VMEM_LIMIT_BYTES),
        name="merge_ffn",
    )(x2, attn, rec, gates, woa, wol, wout, g2, wg, wu, wd, gf)


def _rope_slot_tables(seq):
    pos = jnp.arange(seq, dtype=jnp.float32)
    inv_freq = 1.0 / (ROPE_THETA ** (jnp.arange(0, QK_ROPE_DIM, 2, dtype=jnp.float32) / QK_ROPE_DIM))
    ang = pos[:, None] * inv_freq[None, :]
    cos, sin = jnp.cos(ang), jnp.sin(ang)
    half = QK_ROPE_DIM // 2
    ones = jnp.ones((seq, QK_NOPE_DIM), jnp.float32)
    z_half = jnp.zeros((seq, half), jnp.float32)
    z_nope = jnp.zeros((seq, QK_NOPE_DIM), jnp.float32)
    z_pad = jnp.zeros((seq, HEAD_PAD - QK_DIM), jnp.float32)
    rc = jnp.concatenate([ones, cos, cos, z_pad], axis=1)
    rlo = jnp.concatenate([z_nope, -sin, z_half, z_pad], axis=1)
    rhi = jnp.concatenate([z_nope, z_half, sin, z_pad], axis=1)
    return jnp.concatenate([rc, rlo, rhi], axis=1)


def _head_slots(w, width):
    k = w.shape[0]
    w = w.reshape(k, N_HEADS, width)
    w = jnp.pad(w, ((0, 0), (0, 0), (0, HEAD_PAD - width)))
    return w.reshape(k, N_HEADS * HEAD_PAD)


def kernel(x, norm1_g, w_in, q_a_norm_g, w_q_b, kv_a_norm_g, w_kv_b, w_o_attn,
           conv_w, conv_b, w_rgate, b_rgate, w_igate, b_igate, lru_lambda, w_o_lru,
           w_out, norm2_g, w_ffn_gate, w_ffn_up, w_ffn_down, final_g):
    batch, seq, d = x.shape
    depth = w_in.shape[0]
    bf = jnp.bfloat16
    c_qa = Q_LORA_RANK
    c_kva = c_qa + KV_LORA_RANK
    c_kr = c_kva + QK_ROPE_DIM
    rope = _rope_slot_tables(seq)
    assert depth == 1, "the final rmsnorm is fused into the single layer's last kernel"
    l = 0
    x2 = x.reshape(batch * seq, d)

    wi = w_in[l]
    w_kr = jnp.pad(wi[:, c_kva:c_kr], ((0, 0), (QK_NOPE_DIM, HEAD_PAD - QK_DIM)))
    w_in_r = jnp.concatenate([wi[:, :c_kva], w_kr, wi[:, c_kr:]], axis=1).astype(bf)
    wqb_r = _head_slots(w_q_b[l], QK_DIM).astype(bf)
    wkv = w_kv_b[l].reshape(KV_LORA_RANK, N_HEADS, QK_NOPE_DIM + V_HEAD_DIM)
    wk_r = _head_slots(wkv[:, :, :QK_NOPE_DIM].reshape(KV_LORA_RANK, -1), QK_NOPE_DIM)
    wv = wkv[:, :, QK_NOPE_DIM:].reshape(KV_LORA_RANK, -1)
    wkvb_r = jnp.concatenate([wk_r, wv], axis=1).astype(bf)

    qt, k, vt, lx, gy, gates = _in_proj(
        x2, norm1_g[l][None], w_in_r, q_a_norm_g[l][None], wqb_r, kv_a_norm_g[l][None],
        wkvb_r, rope, batch, seq)

    wg = jnp.concatenate([w_rgate[l, 0], w_rgate[l, 1], w_igate[l, 0], w_igate[l, 1]], axis=2).astype(bf)
    bg = 0.5 * jnp.concatenate([b_rgate[l, 0], b_rgate[l, 1], b_igate[l, 0], b_igate[l, 1]], axis=1)[:, None, :]
    bg_hi = bg.astype(bf)
    bg_lo = (bg - bg_hi.astype(jnp.float32)).astype(bf)
    wg = jnp.concatenate([wg, bg_hi, bg_lo, jnp.zeros((LRU_BLOCKS, LRU_BLOCK_DIM - 2, wg.shape[2]), bf)], axis=1)
    rec = _rglru(lx, gy, conv_w[l], conv_b[l][None], wg, lru_lambda[l], batch, seq)

    attn = _attention(qt, k, vt, batch, seq)

    out = _merge_ffn(x2, attn, rec, gates, w_o_attn[l].astype(bf), w_o_lru[l].astype(bf),
                     w_out[l].astype(bf), norm2_g[l][None], w_ffn_gate[l].astype(bf),
                     w_ffn_up[l].astype(bf), w_ffn_down[l].astype(bf), final_g[None])
    return out.reshape(batch, seq, d)
```
